```python
import math
import jax
import jax.numpy as jnp
from jax import lax
import numpy as np

D_MODEL = 1024
BATCH = 4
SEQ = 4096
DEPTH = 1
DEC_BATCH = 16
DEC_SEQ = 2048
PAST_LEN = 128

DA_HEADS = 4
DA_HEAD_DIM = 64
DA_V_DIM = 2 * DA_HEAD_DIM
DA_WIDTH = DA_HEADS * DA_V_DIM
Q_BLOCK = 128
T5_BUCKETS = 32
T5_MAX_DIST = 128
GRID_W = 64
NA_HEADS = 8
NA_HEAD_DIM = 64
NA_WIDTH = NA_HEADS * NA_HEAD_DIM
NA_ROWS_MAX = 8
NA_COLS = 16
SPLIT_SIZES = (
    DA_HEADS * 2 * DA_HEAD_DIM,
    DA_HEADS * 2 * DA_HEAD_DIM,
    DA_WIDTH,
    DA_WIDTH,
    NA_WIDTH,
    NA_WIDTH,
    NA_WIDTH,
    NA_WIDTH,
    2 * D_MODEL,
)
IN_WIDTH = 4 * DA_WIDTH + 4 * NA_WIDTH + 2 * D_MODEL
NORM_EPS = 1e-6
SUBLN_EPS = 1e-5

kernel_name = "gated_diffattn_natten_encoder"


def rms_norm(x, w, eps=NORM_EPS):
    xf = x.astype(jnp.float32)
    y = xf * lax.rsqrt(jnp.mean(xf * xf, axis=-1, keepdims=True) + eps)
    return (y * w.astype(jnp.float32)).astype(x.dtype)


def t5_bucket(rel):
    nb = T5_BUCKETS // 2
    ret = jnp.where(rel > 0, nb, 0)
    n = jnp.abs(rel)
    max_exact = nb // 2
    nf = jnp.maximum(n, 1).astype(jnp.float32)
    large = max_exact + (jnp.log(nf / max_exact) / math.log(T5_MAX_DIST / max_exact)
                         * (nb - max_exact)).astype(jnp.int32)
    large = jnp.minimum(large, nb - 1)
    return ret + jnp.where(n < max_exact, n, large)


def diff_attention(q, k, v, t5_rel_bias, lam, subln_w, lam_init):
    B, S = q.shape[0], q.shape[1]
    nblk = S // Q_BLOCK
    qs = q * (DA_HEAD_DIM ** -0.5)
    qb = qs.reshape(B, nblk, Q_BLOCK, DA_HEADS, 2, DA_HEAD_DIM).transpose(1, 0, 2, 3, 4, 5)
    kpos = jnp.arange(S, dtype=jnp.int32)

    def block(args):
        qi, bi = args
        qpos = bi * Q_BLOCK + jnp.arange(Q_BLOCK, dtype=jnp.int32)
        bucket = t5_bucket(kpos[None, :] - qpos[:, None])
        bias = jnp.moveaxis(t5_rel_bias[bucket], -1, 0).astype(jnp.float32)
        s = jnp.einsum('bqhmd,bkhmd->bmhqk', qi, k).astype(jnp.float32) + bias[None, None]
        p = jax.nn.softmax(s, axis=-1)
        attn = p[:, 0] - lam * p[:, 1]
        return jnp.einsum('bhqk,bkhe->bqhe', attn.astype(v.dtype), v)

    out = lax.map(block, (qb, jnp.arange(nblk, dtype=jnp.int32)))
    out = out.transpose(1, 0, 2, 3, 4).reshape(B, S, DA_HEADS, DA_V_DIM)
    out = rms_norm(out, subln_w, SUBLN_EPS) * (1.0 - lam_init)
    return out.reshape(B, S, DA_WIDTH)


def neighbourhood_attention(q, k, v, na_rpb):
    B, S = q.shape[0], q.shape[1]
    rows = S // GRID_W
    kh = min(NA_ROWS_MAX, rows)
    qg = (q * (NA_HEAD_DIM ** -0.5)).reshape(B, rows, GRID_W, NA_HEADS, NA_HEAD_DIM)
    qg = qg.transpose(1, 0, 2, 3, 4)
    kg = k.reshape(B, rows, GRID_W, NA_HEADS, NA_HEAD_DIM)
    vg = v.reshape(B, rows, GRID_W, NA_HEADS, NA_HEAD_DIM)
    cols = jnp.arange(GRID_W, dtype=jnp.int32)
    col_start = jnp.clip(cols - NA_COLS // 2, 0, GRID_W - NA_COLS)
    col_idx = col_start[:, None] + jnp.arange(NA_COLS, dtype=jnp.int32)[None, :]
    dc = col_idx - cols[:, None] + (NA_COLS - 1)

    def row(args):
        qr, r = args
        start = jnp.clip(r - kh // 2, 0, rows - kh)
        kb = lax.dynamic_slice_in_dim(kg, start, kh, axis=1)[:, :, col_idx]
        vb = lax.dynamic_slice_in_dim(vg, start, kh, axis=1)[:, :, col_idx]
        dr = start + jnp.arange(kh, dtype=jnp.int32) - r + (NA_ROWS_MAX - 1)
        bias = na_rpb[:, dr[None, :, None], dc[:, None, :]].astype(jnp.float32)
        s = jnp.einsum('bchd,bkcjhd->bhckj', qr, kb).astype(jnp.float32) + bias[None]
        p = jax.nn.softmax(s.reshape(B, NA_HEADS, GRID_W, kh * NA_COLS), axis=-1).reshape(s.shape)
        return jnp.einsum('bhckj,bkcjhd->bchd', p.astype(vb.dtype), vb)

    out = lax.map(row, (qg, jnp.arange(rows, dtype=jnp.int32)))
    return out.transpose(1, 0, 2, 3, 4).reshape(B, S, NA_WIDTH)


def encoder_layer(x, t5_rel_bias, pre_w, post_w, w_in, lq1, lk1, lq2, lk2, subln_w,
                  na_rpb, w_o_diff, w_o_na, w_out, lam_init):
    B, S, D = x.shape
    h = rms_norm(x, pre_w)
    proj = h @ w_in
    split_pts = np.cumsum(np.array(SPLIT_SIZES))[:-1]
    qa, ka, va, za, qn, kn, vn, zn, g = jnp.split(proj, split_pts, axis=-1)
    lam = (jnp.exp(jnp.sum(lq1.astype(jnp.float32) * lk1.astype(jnp.float32)))
           - jnp.exp(jnp.sum(lq2.astype(jnp.float32) * lk2.astype(jnp.float32))) + lam_init)
    oa = diff_attention(qa.reshape(B, S, DA_HEADS, 2, DA_HEAD_DIM),
                        ka.reshape(B, S, DA_HEADS, 2, DA_HEAD_DIM),
                        va.reshape(B, S, DA_HEADS, DA_V_DIM),
                        t5_rel_bias, lam, subln_w, lam_init)
    ya = (oa * jax.nn.silu(za)) @ w_o_diff
    on = neighbourhood_attention(qn.reshape(B, S, NA_HEADS, NA_HEAD_DIM),
                                 kn.reshape(B, S, NA_HEADS, NA_HEAD_DIM),
                                 vn.reshape(B, S, NA_HEADS, NA_HEAD_DIM), na_rpb)
    yn = (on * jax.nn.silu(zn)) @ w_o_na
    gates = jax.nn.sigmoid(g.astype(jnp.float32)).astype(x.dtype).reshape(B, S, 2, D)
    merged = gates[:, :, 0] * ya + gates[:, :, 1] * yn
    out = merged @ w_out
    return x + rms_norm(out, post_w)


def encoder_trunk(x, t5_rel_bias, pre_norm_w, post_norm_w, w_in, lambda_q1, lambda_k1,
                  lambda_q2, lambda_k2, subln_w, na_rpb, w_o_diff, w_o_na, w_out):
    for l in range(DEPTH):
        lam_init = 0.8 - 0.6 * math.exp(-0.3 * l)
        x = encoder_layer(x, t5_rel_bias, pre_norm_w[l], post_norm_w[l], w_in[l],
                          lambda_q1[l], lambda_k1[l], lambda_q2[l], lambda_k2[l], subln_w[l],
                          na_rpb[l], w_o_diff[l], w_o_na[l], w_out[l], lam_init)
    return x


def setup_inputs(seed: int = 0) -> dict:
    key = jax.random.key(seed)
    ks = jax.random.split(key, 15)

    def nrm(k, shape, scale):
        return jax.random.normal(k, shape, jnp.float32) * scale

    return {
        "x_prompt": nrm(ks[0], (BATCH, SEQ, D_MODEL), 1.0),
        "x_sample": nrm(ks[1], (DEC_BATCH, DEC_SEQ, D_MODEL), 1.0),
        "t5_rel_bias": nrm(ks[2], (T5_BUCKETS, DA_HEADS), 0.5),
        "pre_norm_w": 1.0 + nrm(ks[3], (DEPTH, D_MODEL), 0.05),
        "post_norm_w": 1.0 + nrm(ks[4], (DEPTH, D_MODEL), 0.05),
        "w_in": nrm(ks[5], (DEPTH, D_MODEL, IN_WIDTH), D_MODEL ** -0.5),
        "lambda_q1": nrm(ks[6], (DEPTH, DA_HEAD_DIM), 0.1),
        "lambda_k1": nrm(ks[7], (DEPTH, DA_HEAD_DIM), 0.1),
        "lambda_q2": nrm(ks[8], (DEPTH, DA_HEAD_DIM), 0.1),
        "lambda_k2": nrm(ks[9], (DEPTH, DA_HEAD_DIM), 0.1),
        "subln_w": 1.0 + nrm(ks[10], (DEPTH, DA_V_DIM), 0.05),
        "na_rpb": nrm(ks[11], (DEPTH, NA_HEADS, 2 * NA_ROWS_MAX - 1, 2 * NA_COLS - 1), 0.5),
        "w_o_diff": nrm(ks[12], (DEPTH, DA_WIDTH, D_MODEL), DA_WIDTH ** -0.5),
        "w_o_na": nrm(ks[13], (DEPTH, NA_WIDTH, D_MODEL), NA_WIDTH ** -0.5),
        "w_out": nrm(ks[14], (DEPTH, D_MODEL, D_MODEL), D_MODEL ** -0.5),
    }


def reference(x_prompt, x_sample, t5_rel_bias, pre_norm_w, post_norm_w, w_in, lambda_q1,
              lambda_k1, lambda_q2, lambda_k2, subln_w, na_rpb, w_o_diff, w_o_na, w_out):
    y_prompt = encoder_trunk(x_prompt, t5_rel_bias, pre_norm_w, post_norm_w, w_in, lambda_q1,
                             lambda_k1, lambda_q2, lambda_k2, subln_w, na_rpb, w_o_diff,
                             w_o_na, w_out)
    y_sample = encoder_trunk(x_sample, t5_rel_bias, pre_norm_w, post_norm_w, w_in, lambda_q1,
                             lambda_k1, lambda_q2, lambda_k2, subln_w, na_rpb, w_o_diff,
                             w_o_na, w_out)
    return (y_prompt, y_sample)
```

```python
import functools
import math

import jax
import jax.numpy as jnp
import numpy as np
from jax import lax
from jax.experimental import pallas as pl
from jax.experimental.pallas import tpu as pltpu

D_MODEL = 1024
DA_HEADS = 4
DA_HEAD_DIM = 64
DA_V_DIM = 2 * DA_HEAD_DIM
DA_WIDTH = DA_HEADS * DA_V_DIM
T5_BUCKETS = 32
T5_MAX_DIST = 128
GRID_W = 64
NA_HEADS = 8
NA_HEAD_DIM = 64
NA_WIDTH = NA_HEADS * NA_HEAD_DIM
NA_ROWS = 8
NA_COLS = 16
IN_WIDTH = 4 * DA_WIDTH + 4 * NA_WIDTH + 2 * D_MODEL
NORM_EPS = 1e-6
SUBLN_EPS = 1e-5
LOG2E = math.log2(math.e)

LANES = 128
VMEM_LIMIT_BYTES = 56 * 1024 * 1024

PROJ_TM = 512
PROJ_CHUNK = 512
DA_T = 256
NA_RB = 8
OUT_TM = 512
NEG_BIG = -1e30

_QA, _KA, _VA = 0, DA_HEADS, 2 * DA_HEADS
_ZA_BLK, _QN_BLK, _KN_BLK, _VN_BLK, _ZN_BLK = 3, 4, 5, 6, 7
_G_BLK = 2

_NT = (((1,), (1,)), ((), ()))


def _params(sem):
    return pltpu.CompilerParams(dimension_semantics=sem, vmem_limit_bytes=VMEM_LIMIT_BYTES)


def _proj_kernel(x_ref, pw_ref, w_ref, o_ref, *, scales):
    x = x_ref[...]
    ms = jnp.mean(x * x, axis=-1, keepdims=True)
    h = (x * lax.rsqrt(ms + NORM_EPS) * pw_ref[...]).astype(jnp.bfloat16)
    for c, sc in enumerate(scales):
        cols = slice(c * PROJ_CHUNK, (c + 1) * PROJ_CHUNK)
        acc = jnp.dot(h, w_ref[:, cols], preferred_element_type=jnp.float32)
        if sc != 1.0:
            acc = acc * sc
        o_ref[:, cols] = acc.astype(jnp.bfloat16)


def _proj(x2d, pre_w, w_in_bf16):
    tokens = x2d.shape[0]
    assert tokens % PROJ_TM == 0
    qscale = LOG2E * DA_HEAD_DIM ** -0.5
    assert DA_HEAD_DIM == NA_HEAD_DIM
    scales = [1.0] * (IN_WIDTH // PROJ_CHUNK)
    scales[0] = qscale
    scales[(4 * DA_WIDTH) // PROJ_CHUNK] = qscale
    assert DA_WIDTH == PROJ_CHUNK and NA_WIDTH == PROJ_CHUNK
    return pl.pallas_call(
        functools.partial(_proj_kernel, scales=tuple(scales)),
        grid=(tokens // PROJ_TM,),
        in_specs=[
            pl.BlockSpec((PROJ_TM, D_MODEL), lambda i: (i, 0)),
            pl.BlockSpec((1, D_MODEL), lambda i: (0, 0)),
            pl.BlockSpec((D_MODEL, IN_WIDTH), lambda i: (0, 0)),
        ],
        out_specs=pl.BlockSpec((PROJ_TM, IN_WIDTH), lambda i: (i, 0)),
        out_shape=jax.ShapeDtypeStruct((tokens, IN_WIDTH), jnp.bfloat16),
        compiler_params=_params(("parallel",)),
        name="proj",
    )(x2d, pre_w, w_in_bf16)


def _da_kernel(lam_ref, cb_ref, q_ref, k_ref, v_ref, bias_ref, sw_ref, o_ref,
               m_scr, l_scr, acc_scr, *, n_chunks, out_scale):
    T = DA_T
    h = pl.program_id(1)
    qi = pl.program_id(2)
    q = q_ref[0]
    lane = lax.broadcasted_iota(jnp.int32, (T, LANES), 1)
    zero = jnp.zeros_like(q)
    qm = (jnp.where(lane < DA_HEAD_DIM, q, zero), jnp.where(lane >= DA_HEAD_DIM, q, zero))

    m_scr[...] = jnp.full(m_scr.shape, -jnp.inf, jnp.float32)
    l_scr[...] = jnp.zeros(l_scr.shape, jnp.float32)
    acc_scr[...] = jnp.zeros(acc_scr.shape, jnp.float32)

    def chunk(kj, bias):
        rows = pl.ds(pl.multiple_of(kj * T, T), T)
        k = k_ref[0, rows, :]
        v = v_ref[0, rows, :]
        for m in range(2):
            s = lax.dot_general(qm[m], k, _NT, preferred_element_type=jnp.float32) + bias
            m_prev = m_scr[m]
            m_new = jnp.maximum(m_prev, jnp.max(s, axis=-1, keepdims=True))
            alpha = jnp.exp2(m_prev - m_new)
            p = jnp.exp2(s - m_new)
            l_scr[m] = alpha * l_scr[m] + jnp.sum(p, axis=-1, keepdims=True)
            acc_scr[m] = alpha * acc_scr[m] + jnp.dot(
                p.astype(jnp.bfloat16), v, preferred_element_type=jnp.float32)
            m_scr[m] = m_new

    c_lo = cb_ref[h, 0]
    c_hi = cb_ref[h, 1]

    def far_lo(kj, carry):
        chunk(kj, c_lo)
        return carry

    def far_hi(kj, carry):
        chunk(kj, c_hi)
        return carry

    lax.fori_loop(0, jnp.maximum(qi - 1, 0), far_lo, 0)

    @pl.when(qi > 0)
    def _():
        chunk(qi - 1, bias_ref[0, 0])

    chunk(qi, bias_ref[0, 1])

    @pl.when(qi < n_chunks - 1)
    def _():
        chunk(qi + 1, bias_ref[0, 2])

    lax.fori_loop(qi + 2, n_chunks, far_hi, 0)

    lam = lam_ref[0]
    o = acc_scr[0] / l_scr[0] - lam * (acc_scr[1] / l_scr[1])
    ms = jnp.mean(o * o, axis=-1, keepdims=True)
    o = o * lax.rsqrt(ms + SUBLN_EPS) * sw_ref[...] * out_scale
    o_ref[0] = o.astype(jnp.bfloat16)


def _diff_attention(proj, lam, cb, bias_tiles, subln_w, lam_init):
    B, S, _ = proj.shape
    T = DA_T
    assert S % T == 0 and T >= T5_MAX_DIST
    n = S // T
    return pl.pallas_call(
        functools.partial(_da_kernel, n_chunks=n, out_scale=1.0 - lam_init),
        grid=(B, DA_HEADS, n),
        in_specs=[
            pl.BlockSpec(memory_space=pltpu.SMEM),
            pl.BlockSpec(memory_space=pltpu.SMEM),
            pl.BlockSpec((1, T, LANES), lambda b, h, i: (b, i, _QA + h)),
            pl.BlockSpec((1, S, LANES), lambda b, h, i: (b, 0, _KA + h)),
            pl.BlockSpec((1, S, LANES), lambda b, h, i: (b, 0, _VA + h)),
            pl.BlockSpec((1, 3, T, T), lambda b, h, i: (h, 0, 0, 0)),
            pl.BlockSpec((1, DA_V_DIM), lambda b, h, i: (0, 0)),
        ],
        out_specs=pl.BlockSpec((1, T, LANES), lambda b, h, i: (b, i, h)),
        out_shape=jax.ShapeDtypeStruct((B, S, DA_WIDTH), jnp.bfloat16),
        scratch_shapes=[
            pltpu.VMEM((2, T, 1), jnp.float32),
            pltpu.VMEM((2, T, 1), jnp.float32),
            pltpu.VMEM((2, T, DA_V_DIM), jnp.float32),
        ],
        compiler_params=_params(("parallel", "parallel", "parallel")),
        name="diff_attn",
    )(lam, cb, proj, proj, proj, bias_tiles, subln_w)


def _na_kernel(q_ref, k_ref, v_ref, bias_ref, o_ref, *, rows):
    rb = pl.program_id(1)
    win = NA_ROWS * GRID_W
    lane = lax.broadcasted_iota(jnp.int32, (GRID_W, LANES), 1)
    first = lane < NA_HEAD_DIM

    def row_body(i, carry):
        r = rb * NA_RB + i
        start = jnp.clip(r - NA_ROWS // 2, 0, rows - NA_ROWS)
        var = r - start
        krows = pl.ds(pl.multiple_of(start * GRID_W, GRID_W), win)
        qrows = pl.ds(pl.multiple_of(i * GRID_W, GRID_W), GRID_W)
        for hp in range(NA_HEADS // 2):
            cols = slice(hp * LANES, (hp + 1) * LANES)
            q2 = q_ref[0, qrows, cols]
            k2 = k_ref[0, krows, cols]
            v2 = v_ref[0, krows, cols]
            zero = jnp.zeros_like(q2)
            outs = []
            for e in range(2):
                qe = jnp.where(first if e == 0 else jnp.logical_not(first), q2, zero)
                s = lax.dot_general(qe, k2, _NT, preferred_element_type=jnp.float32)
                s = s + bias_ref[var, 2 * hp + e]
                m = jnp.max(s, axis=-1, keepdims=True)
                p = jnp.exp2(s - m)
                l = jnp.sum(p, axis=-1, keepdims=True)
                pv = jnp.dot(p.astype(jnp.bfloat16), v2, preferred_element_type=jnp.float32)
                outs.append(pv / l)
            o_ref[0, qrows, cols] = jnp.where(first, outs[0], outs[1]).astype(jnp.bfloat16)
        return carry

    lax.fori_loop(0, NA_RB, row_body, 0)


def _neighbourhood_attention(proj, na_bias):
    B, S, _ = proj.shape
    rows = S // GRID_W
    assert S % GRID_W == 0 and rows >= NA_ROWS and rows % NA_RB == 0
    blk = NA_RB * GRID_W
    return pl.pallas_call(
        functools.partial(_na_kernel, rows=rows),
        grid=(B, rows // NA_RB),
        in_specs=[
            pl.BlockSpec((1, blk, NA_WIDTH), lambda b, i: (b, i, _QN_BLK)),
            pl.BlockSpec((1, S, NA_WIDTH), lambda b, i: (b, 0, _KN_BLK)),
            pl.BlockSpec((1, S, NA_WIDTH), lambda b, i: (b, 0, _VN_BLK)),
            pl.BlockSpec(na_bias.shape, lambda b, i: (0, 0, 0, 0)),
        ],
        out_specs=pl.BlockSpec((1, blk, NA_WIDTH), lambda b, i: (b, i, 0)),
        out_shape=jax.ShapeDtypeStruct((B, S, NA_WIDTH), jnp.bfloat16),
        compiler_params=_params(("parallel", "parallel")),
        name="nbr_attn",
    )(proj, proj, proj, na_bias)


def _out_kernel(x_ref, oa_ref, on_ref, za_ref, zn_ref, g_ref, woa_ref, won_ref, wout_ref,
                pw_ref, y_ref):
    def branch(o_ref, z_ref, w_ref):
        z = z_ref[...].astype(jnp.float32)
        u = o_ref[...].astype(jnp.float32) * (z * jax.nn.sigmoid(z))
        return jnp.dot(u.astype(jnp.bfloat16), w_ref[...], preferred_element_type=jnp.float32)

    ya = branch(oa_ref, za_ref, woa_ref)
    yn = branch(on_ref, zn_ref, won_ref)
    ga = jax.nn.sigmoid(g_ref[:, :D_MODEL].astype(jnp.float32))
    gn = jax.nn.sigmoid(g_ref[:, D_MODEL:].astype(jnp.float32))
    merged = (ga * ya + gn * yn).astype(jnp.bfloat16)
    out = jnp.dot(merged, wout_ref[...], preferred_element_type=jnp.float32)
    ms = jnp.mean(out * out, axis=-1, keepdims=True)
    y_ref[...] = x_ref[...] + out * lax.rsqrt(ms + NORM_EPS) * pw_ref[...]


def _output(x2d, oa2d, on2d, proj2d, w_o_diff, w_o_na, w_out, post_w):
    tokens = x2d.shape[0]
    tm = OUT_TM
    assert tokens % tm == 0

    def const(shape):
        return pl.BlockSpec(shape, lambda i: (0, 0))

    return pl.pallas_call(
        _out_kernel,
        grid=(tokens // tm,),
        in_specs=[
            pl.BlockSpec((tm, D_MODEL), lambda i: (i, 0)),
            pl.BlockSpec((tm, DA_WIDTH), lambda i: (i, 0)),
            pl.BlockSpec((tm, NA_WIDTH), lambda i: (i, 0)),
            pl.BlockSpec((tm, DA_WIDTH), lambda i: (i, _ZA_BLK)),
            pl.BlockSpec((tm, NA_WIDTH), lambda i: (i, _ZN_BLK)),
            pl.BlockSpec((tm, 2 * D_MODEL), lambda i: (i, _G_BLK)),
            const((DA_WIDTH, D_MODEL)),
            const((NA_WIDTH, D_MODEL)),
            const((D_MODEL, D_MODEL)),
            const((1, D_MODEL)),
        ],
        out_specs=pl.BlockSpec((tm, D_MODEL), lambda i: (i, 0)),
        out_shape=jax.ShapeDtypeStruct((tokens, D_MODEL), jnp.float32),
        compiler_params=_params(("parallel",)),
        name="out_proj",
    )(x2d, oa2d, on2d, proj2d, proj2d, proj2d, w_o_diff, w_o_na, w_out, post_w)


def _t5_bucket(rel):
    nb = T5_BUCKETS // 2
    ret = jnp.where(rel > 0, nb, 0)
    n = jnp.abs(rel)
    max_exact = nb // 2
    nf = jnp.maximum(n, 1).astype(jnp.float32)
    large = max_exact + (jnp.log(nf / max_exact) / math.log(T5_MAX_DIST / max_exact)
                         * (nb - max_exact)).astype(jnp.int32)
    large = jnp.minimum(large, nb - 1)
    return ret + jnp.where(n < max_exact, n, large)


def _t5_tables(t5_rel_bias):
    T = DA_T
    i = jnp.arange(T, dtype=jnp.int32)
    rel = (jnp.arange(-1, 2, dtype=jnp.int32) * T)[:, None, None] + i[None, None, :] - i[None, :, None]
    tiles = t5_rel_bias[_t5_bucket(rel)].astype(jnp.float32) * LOG2E
    far = jnp.array([-T5_MAX_DIST, T5_MAX_DIST], dtype=jnp.int32)
    cb = t5_rel_bias[_t5_bucket(far)].astype(jnp.float32) * LOG2E
    return jnp.transpose(tiles, (3, 0, 1, 2)), jnp.transpose(cb, (1, 0))


def _na_tables(na_rpb):
    var = jnp.arange(NA_ROWS, dtype=jnp.int32)[:, None, None, None]
    c = jnp.arange(GRID_W, dtype=jnp.int32)[None, :, None, None]
    kr = jnp.arange(NA_ROWS, dtype=jnp.int32)[None, None, :, None]
    kc = jnp.arange(GRID_W, dtype=jnp.int32)[None, None, None, :]
    cs = jnp.clip(c - NA_COLS // 2, 0, GRID_W - NA_COLS)
    valid = (kc >= cs) & (kc < cs + NA_COLS)
    dr = kr + (NA_ROWS - 1) - var
    dc = jnp.clip(kc - c + (NA_COLS - 1), 0, 2 * NA_COLS - 2)
    vals = na_rpb[:, dr, dc].astype(jnp.float32) * LOG2E
    vals = jnp.where(valid[None], vals, NEG_BIG)
    vals = jnp.transpose(vals, (1, 0, 2, 3, 4))
    return vals.reshape(NA_ROWS, NA_HEADS, GRID_W, NA_ROWS * GRID_W)


def _layer(x, tables, pre_w, post_w, w_in, lam, subln_w, w_o_diff, w_o_na, w_out, lam_init):
    B, S, D = x.shape
    t5_tiles, t5_far, na_bias = tables
    x2d = x.reshape(B * S, D)
    proj2d = _proj(x2d, pre_w, w_in)
    proj = proj2d.reshape(B, S, IN_WIDTH)
    oa = _diff_attention(proj, lam, t5_far, t5_tiles, subln_w, lam_init)
    on = _neighbourhood_attention(proj, na_bias)
    y = _output(x2d, oa.reshape(B * S, DA_WIDTH), on.reshape(B * S, NA_WIDTH), proj2d,
                w_o_diff, w_o_na, w_out, post_w)
    return y.reshape(B, S, D)


def _trunk(xs, t5_rel_bias, pre_norm_w, post_norm_w, w_in, lambda_q1, lambda_k1, lambda_q2,
           lambda_k2, subln_w, na_rpb, w_o_diff, w_o_na, w_out):
    depth = w_in.shape[0]
    t5_tiles, t5_far = _t5_tables(t5_rel_bias)
    f32 = jnp.float32
    for l in range(depth):
        lam_init = 0.8 - 0.6 * math.exp(-0.3 * l)
        lam = (jnp.exp(jnp.sum(lambda_q1[l].astype(f32) * lambda_k1[l].astype(f32)))
               - jnp.exp(jnp.sum(lambda_q2[l].astype(f32) * lambda_k2[l].astype(f32)))
               + lam_init).reshape(1)
        tables = (t5_tiles, t5_far, _na_tables(na_rpb[l]))
        args = (pre_norm_w[l].reshape(1, D_MODEL).astype(f32),
                post_norm_w[l].reshape(1, D_MODEL).astype(f32),
                w_in[l].astype(jnp.bfloat16), lam,
                subln_w[l].reshape(1, DA_V_DIM).astype(f32),
                w_o_diff[l].astype(jnp.bfloat16), w_o_na[l].astype(jnp.bfloat16),
                w_out[l].astype(jnp.bfloat16), lam_init)
        xs = [_layer(x, tables, *args) for x in xs]
    return xs


def kernel(x_prompt, x_sample, t5_rel_bias, pre_norm_w, post_norm_w, w_in, lambda_q1, lambda_k1,
           lambda_q2, lambda_k2, subln_w, na_rpb, w_o_diff, w_o_na, w_out):
    y_prompt, y_sample = _trunk([x_prompt, x_sample], t5_rel_bias, pre_norm_w, post_norm_w, w_in,
                                lambda_q1, lambda_k1, lambda_q2, lambda_k2, subln_w, na_rpb,
                                w_o_diff, w_o_na, w_out)
    return (y_prompt, y_sample)
```

```python
import functools
import math

import jax
import jax.numpy as jnp
import numpy as np
from jax import lax
from jax.experimental import pallas as pl
from jax.experimental.pallas import tpu as pltpu

D_MODEL = 1024
DA_HEADS = 4
DA_HEAD_DIM = 64
DA_V_DIM = 2 * DA_HEAD_DIM
DA_WIDTH = DA_HEADS * DA_V_DIM
T5_BUCKETS = 32
T5_MAX_DIST = 128
GRID_W = 64
NA_HEADS = 8
NA_HEAD_DIM = 64
NA_WIDTH = NA_HEADS * NA_HEAD_DIM
NA_ROWS = 8
NA_COLS = 16
IN_WIDTH = 4 * DA_WIDTH + 4 * NA_WIDTH + 2 * D_MODEL
NORM_EPS = 1e-6
SUBLN_EPS = 1e-5
LOG2E = math.log2(math.e)

LANES = 128
VMEM_LIMIT_BYTES = 56 * 1024 * 1024

PROJ_TM = 512
PROJ_CHUNK = 512
DA_T = 256
DA_ROWS = 64
DA_BIAS_TILES = 5
NA_RB = 8
NA_GROUP = 4
OUT_TM = 512
NEG_BIG = -1e30

_QA, _KA, _VA = 0, DA_HEADS, 2 * DA_HEADS
_ZA_BLK, _QN_BLK, _KN_BLK, _VN_BLK, _ZN_BLK = 3, 4, 5, 6, 7
_G_BLK = 2

_NT = (((1,), (1,)), ((), ()))


def _params(sem):
    return pltpu.CompilerParams(dimension_semantics=sem, vmem_limit_bytes=VMEM_LIMIT_BYTES)


def _proj_kernel(x_ref, pw_ref, w_ref, o_ref, *, scales):
    x = x_ref[...]
    ms = jnp.mean(x * x, axis=-1, keepdims=True)
    h = (x * lax.rsqrt(ms + NORM_EPS) * pw_ref[...]).astype(jnp.bfloat16)
    for c, sc in enumerate(scales):
        cols = slice(c * PROJ_CHUNK, (c + 1) * PROJ_CHUNK)
        acc = jnp.dot(h, w_ref[:, cols], preferred_element_type=jnp.float32)
        if sc != 1.0:
            acc = acc * sc
        o_ref[:, cols] = acc.astype(jnp.bfloat16)


def _proj(x2d, pre_w, w_in_bf16):
    tokens = x2d.shape[0]
    assert tokens % PROJ_TM == 0
    qscale = LOG2E * DA_HEAD_DIM ** -0.5
    assert DA_HEAD_DIM == NA_HEAD_DIM
    scales = [1.0] * (IN_WIDTH // PROJ_CHUNK)
    scales[0] = qscale
    scales[(4 * DA_WIDTH) // PROJ_CHUNK] = qscale
    assert DA_WIDTH == PROJ_CHUNK and NA_WIDTH == PROJ_CHUNK
    return pl.pallas_call(
        functools.partial(_proj_kernel, scales=tuple(scales)),
        grid=(tokens // PROJ_TM,),
        in_specs=[
            pl.BlockSpec((PROJ_TM, D_MODEL), lambda i: (i, 0)),
            pl.BlockSpec((1, D_MODEL), lambda i: (0, 0)),
            pl.BlockSpec((D_MODEL, IN_WIDTH), lambda i: (0, 0)),
        ],
        out_specs=pl.BlockSpec((PROJ_TM, IN_WIDTH), lambda i: (i, 0)),
        out_shape=jax.ShapeDtypeStruct((tokens, IN_WIDTH), jnp.bfloat16),
        compiler_params=_params(("parallel",)),
        name="proj",
    )(x2d, pre_w, w_in_bf16)


def _lane_groups(x):
    return [x[:, g * LANES:(g + 1) * LANES] for g in range(x.shape[1] // LANES)]


def _da_kernel(lam_ref, q_ref, k_ref, v_ref, bias_ref, sw_ref, o_ref,
               s_scr, p_scr, mx_scr, st_scr, *, n_chunks, out_scale):
    T = DA_T
    R = DA_ROWS
    h = pl.program_id(1)
    qi = pl.program_id(2)
    q = q_ref[0]
    lane = lax.broadcasted_iota(jnp.int32, (T, LANES), 1)
    zero = jnp.zeros_like(q)
    q2 = jnp.concatenate([jnp.where(lane < DA_HEAD_DIM, q, zero),
                          jnp.where(lane >= DA_HEAD_DIM, q, zero)], axis=0)

    mx_scr[...] = jnp.full(mx_scr.shape, -jnp.inf, jnp.float32)
    far = (DA_BIAS_TILES - 1) // 2
    for c in range(n_chunks):
        bias = bias_ref[0, jnp.clip(c - qi, -far, far) + far]
        s2 = lax.dot_general(q2, k_ref[0, c * T:(c + 1) * T, :], _NT,
                             preferred_element_type=jnp.float32)
        for m in range(2):
            s = s2[m * T:(m + 1) * T] + bias
            s_scr[m, c] = s
            groups = _lane_groups(s)
            gmax = groups[0]
            for sg in groups[1:]:
                gmax = jnp.maximum(gmax, sg)
            mx_scr[m] = jnp.maximum(mx_scr[m], gmax)

    for m in range(2):
        row_max = jnp.max(mx_scr[m], axis=-1, keepdims=True)
        st_scr[m] = jnp.broadcast_to(row_max, (T, LANES))

    def exp_rows(rb, carry):
        rows = pl.ds(pl.multiple_of(rb * R, R), R)
        for m in range(2):
            mb = st_scr[m, rows, :]
            mbt = jnp.concatenate([mb] * (T // LANES), axis=1)
            l = jnp.zeros((R, LANES), jnp.float32)
            for c in range(n_chunks):
                p = jnp.exp2(s_scr[m, c, rows, :] - mbt)
                for pg in _lane_groups(p):
                    l = l + pg
                p_scr[m, c, rows, :] = p.astype(jnp.bfloat16)
            st_scr[m, rows, :] = l
        return carry

    lax.fori_loop(0, T // R, exp_rows, 0)

    l0 = jnp.sum(st_scr[0], axis=-1, keepdims=True)
    l1 = jnp.sum(st_scr[1], axis=-1, keepdims=True)
    ratio = jnp.broadcast_to(lam_ref[0] * l0 / l1, (T, LANES)).astype(jnp.bfloat16)
    ratio = jnp.concatenate([ratio] * (T // LANES), axis=1)
    acc = jnp.zeros((T, DA_V_DIM), jnp.float32)
    for c in range(n_chunks):
        w = p_scr[0, c] - ratio * p_scr[1, c]
        acc = acc + jnp.dot(w, v_ref[0, c * T:(c + 1) * T, :], preferred_element_type=jnp.float32)

    o = acc / l0
    ms = jnp.mean(o * o, axis=-1, keepdims=True)
    o = o * lax.rsqrt(ms + SUBLN_EPS) * sw_ref[...] * out_scale
    o_ref[0] = o.astype(jnp.bfloat16)


def _diff_attention(proj, lam, bias_tiles, subln_w, lam_init):
    B, S, _ = proj.shape
    T = DA_T
    assert S % T == 0 and T >= T5_MAX_DIST
    n = S // T
    return pl.pallas_call(
        functools.partial(_da_kernel, n_chunks=n, out_scale=1.0 - lam_init),
        grid=(B, DA_HEADS, n),
        in_specs=[
            pl.BlockSpec(memory_space=pltpu.SMEM),
            pl.BlockSpec((1, T, LANES), lambda b, h, i: (b, i, _QA + h)),
            pl.BlockSpec((1, S, LANES), lambda b, h, i: (b, 0, _KA + h)),
            pl.BlockSpec((1, S, LANES), lambda b, h, i: (b, 0, _VA + h)),
            pl.BlockSpec((1, DA_BIAS_TILES, T, T), lambda b, h, i: (h, 0, 0, 0)),
            pl.BlockSpec((1, DA_V_DIM), lambda b, h, i: (0, 0)),
        ],
        out_specs=pl.BlockSpec((1, T, LANES), lambda b, h, i: (b, i, h)),
        out_shape=jax.ShapeDtypeStruct((B, S, DA_WIDTH), jnp.bfloat16),
        scratch_shapes=[
            pltpu.VMEM((2, n, T, T), jnp.float32),
            pltpu.VMEM((2, n, T, T), jnp.bfloat16),
            pltpu.VMEM((2, T, LANES), jnp.float32),
            pltpu.VMEM((2, T, LANES), jnp.float32),
        ],
        compiler_params=_params(("parallel", "parallel", "parallel")),
        name="diff_attn",
    )(lam, proj, proj, proj, bias_tiles, subln_w)


def _na_kernel(q_ref, k_ref, v_ref, bias_ref, o_ref, *, rows):
    rb = pl.program_id(1)
    win = NA_ROWS * GRID_W
    gw = NA_GROUP * NA_HEAD_DIM
    head_of_lane = lax.broadcasted_iota(jnp.int32, (GRID_W, gw), 1) // NA_HEAD_DIM

    def row_body(i, carry):
        r = rb * NA_RB + i
        start = jnp.clip(r - NA_ROWS // 2, 0, rows - NA_ROWS)
        var = r - start
        krows = pl.ds(pl.multiple_of(start * GRID_W, GRID_W), win)
        qrows = pl.ds(pl.multiple_of(i * GRID_W, GRID_W), GRID_W)
        for g in range(NA_HEADS // NA_GROUP):
            cols = slice(g * gw, (g + 1) * gw)
            qg = q_ref[0, qrows, cols]
            zero = jnp.zeros_like(qg)
            qbd = jnp.concatenate([jnp.where(head_of_lane == h, qg, zero) for h in range(NA_GROUP)],
                                  axis=0)
            s = lax.dot_general(qbd, k_ref[0, krows, cols], _NT, preferred_element_type=jnp.float32)
            s = s + bias_ref[var, g]
            m = jnp.max(s, axis=-1, keepdims=True)
            p = jnp.exp2(s - m)
            l = jnp.sum(p, axis=-1, keepdims=True)
            pv = jnp.dot(p.astype(jnp.bfloat16), v_ref[0, krows, cols],
                         preferred_element_type=jnp.float32) / l
            o = pv[:GRID_W]
            for h in range(1, NA_GROUP):
                o = jnp.where(head_of_lane == h, pv[h * GRID_W:(h + 1) * GRID_W], o)
            o_ref[0, qrows, cols] = o.astype(jnp.bfloat16)
        return carry

    lax.fori_loop(0, NA_RB, row_body, 0, unroll=2)


def _neighbourhood_attention(proj, na_bias):
    B, S, _ = proj.shape
    rows = S // GRID_W
    assert S % GRID_W == 0 and rows >= NA_ROWS and rows % NA_RB == 0
    blk = NA_RB * GRID_W
    return pl.pallas_call(
        functools.partial(_na_kernel, rows=rows),
        grid=(B, rows // NA_RB),
        in_specs=[
            pl.BlockSpec((1, blk, NA_WIDTH), lambda b, i: (b, i, _QN_BLK)),
            pl.BlockSpec((1, S, NA_WIDTH), lambda b, i: (b, 0, _KN_BLK)),
            pl.BlockSpec((1, S, NA_WIDTH), lambda b, i: (b, 0, _VN_BLK)),
            pl.BlockSpec(na_bias.shape, lambda b, i: (0, 0, 0, 0)),
        ],
        out_specs=pl.BlockSpec((1, blk, NA_WIDTH), lambda b, i: (b, i, 0)),
        out_shape=jax.ShapeDtypeStruct((B, S, NA_WIDTH), jnp.bfloat16),
        compiler_params=_params(("parallel", "parallel")),
        name="nbr_attn",
    )(proj, proj, proj, na_bias)


def _out_kernel(x_ref, oa_ref, on_ref, za_ref, zn_ref, g_ref, woa_ref, won_ref, wout_ref,
                pw_ref, y_ref):
    def branch(o_ref, z_ref, w_ref):
        z = z_ref[...].astype(jnp.float32)
        u = o_ref[...].astype(jnp.float32) * (z * jax.nn.sigmoid(z))
        return jnp.dot(u.astype(jnp.bfloat16), w_ref[...], preferred_element_type=jnp.float32)

    ya = branch(oa_ref, za_ref, woa_ref)
    yn = branch(on_ref, zn_ref, won_ref)
    ga = jax.nn.sigmoid(g_ref[:, :D_MODEL].astype(jnp.float32))
    gn = jax.nn.sigmoid(g_ref[:, D_MODEL:].astype(jnp.float32))
    merged = (ga * ya + gn * yn).astype(jnp.bfloat16)
    out = jnp.dot(merged, wout_ref[...], preferred_element_type=jnp.float32)
    ms = jnp.mean(out * out, axis=-1, keepdims=True)
    y_ref[...] = x_ref[...] + out * lax.rsqrt(ms + NORM_EPS) * pw_ref[...]


def _output(x2d, oa2d, on2d, proj2d, w_o_diff, w_o_na, w_out, post_w):
    tokens = x2d.shape[0]
    tm = OUT_TM
    assert tokens % tm == 0

    def const(shape):
        return pl.BlockSpec(shape, lambda i: (0, 0))

    return pl.pallas_call(
        _out_kernel,
        grid=(tokens // tm,),
        in_specs=[
            pl.BlockSpec((tm, D_MODEL), lambda i: (i, 0)),
            pl.BlockSpec((tm, DA_WIDTH), lambda i: (i, 0)),
            pl.BlockSpec((tm, NA_WIDTH), lambda i: (i, 0)),
            pl.BlockSpec((tm, DA_WIDTH), lambda i: (i, _ZA_BLK)),
            pl.BlockSpec((tm, NA_WIDTH), lambda i: (i, _ZN_BLK)),
            pl.BlockSpec((tm, 2 * D_MODEL), lambda i: (i, _G_BLK)),
            const((DA_WIDTH, D_MODEL)),
            const((NA_WIDTH, D_MODEL)),
            const((D_MODEL, D_MODEL)),
            const((1, D_MODEL)),
        ],
        out_specs=pl.BlockSpec((tm, D_MODEL), lambda i: (i, 0)),
        out_shape=jax.ShapeDtypeStruct((tokens, D_MODEL), jnp.float32),
        compiler_params=_params(("parallel",)),
        name="out_proj",
    )(x2d, oa2d, on2d, proj2d, proj2d, proj2d, w_o_diff, w_o_na, w_out, post_w)


def _t5_bucket(rel):
    nb = T5_BUCKETS // 2
    ret = jnp.where(rel > 0, nb, 0)
    n = jnp.abs(rel)
    max_exact = nb // 2
    nf = jnp.maximum(n, 1).astype(jnp.float32)
    large = max_exact + (jnp.log(nf / max_exact) / math.log(T5_MAX_DIST / max_exact)
                         * (nb - max_exact)).astype(jnp.int32)
    large = jnp.minimum(large, nb - 1)
    return ret + jnp.where(n < max_exact, n, large)


def _toeplitz(w, rows, first, cols):
    M = w.shape[-1]
    assert first - (rows - 1) >= 0 and first + cols <= M - 1
    lead = w.shape[:-1]
    flat = jnp.tile(w, (1,) * len(lead) + (rows,))[..., :rows * (M - 1)]
    return flat.reshape(lead + (rows, M - 1))[..., first:first + cols]


def _t5_tables(t5_rel_bias):
    T = DA_T
    far = (DA_BIAS_TILES - 1) // 2
    M = (2 * far + 2) * T + 8
    rel = jnp.arange(M, dtype=jnp.int32) - (far + 1) * T
    vec = jnp.transpose(t5_rel_bias[_t5_bucket(rel)].astype(jnp.float32) * LOG2E)
    return jnp.stack([_toeplitz(vec, T, (d + far + 1) * T, T) for d in range(-far, far + 1)], axis=1)


def _na_tables(na_rpb):
    n_dr, n_dc = 2 * NA_ROWS - 1, 2 * NA_COLS - 1
    M = 2 * GRID_W
    pad = GRID_W - NA_COLS
    ring = jnp.pad(na_rpb.astype(jnp.float32) * LOG2E, ((0, 0), (0, 0), (pad, M - pad - n_dc)))
    band = _toeplitz(ring, GRID_W, GRID_W - 1, GRID_W)
    c = jnp.arange(GRID_W, dtype=jnp.int32)[:, None]
    kc = jnp.arange(GRID_W, dtype=jnp.int32)[None, :]
    cs = jnp.clip(c - NA_COLS // 2, 0, GRID_W - NA_COLS)
    band = jnp.where((kc >= cs) & (kc < cs + NA_COLS), band, NEG_BIG)
    per_var = [jnp.transpose(band[:, NA_ROWS - 1 - v:n_dr - v], (0, 2, 1, 3)) for v in range(NA_ROWS)]
    return jnp.stack(per_var, axis=0).reshape(NA_ROWS, NA_HEADS // NA_GROUP, NA_GROUP * GRID_W,
                                              NA_ROWS * GRID_W)


def _layer(x, tables, pre_w, post_w, w_in, lam, subln_w, w_o_diff, w_o_na, w_out, lam_init):
    B, S, D = x.shape
    t5_tiles, na_bias = tables
    x2d = x.reshape(B * S, D)
    proj2d = _proj(x2d, pre_w, w_in)
    proj = proj2d.reshape(B, S, IN_WIDTH)
    oa = _diff_attention(proj, lam, t5_tiles, subln_w, lam_init)
    on = _neighbourhood_attention(proj, na_bias)
    y = _output(x2d, oa.reshape(B * S, DA_WIDTH), on.reshape(B * S, NA_WIDTH), proj2d,
                w_o_diff, w_o_na, w_out, post_w)
    return y.reshape(B, S, D)


def _trunk(xs, t5_rel_bias, pre_norm_w, post_norm_w, w_in, lambda_q1, lambda_k1, lambda_q2,
           lambda_k2, subln_w, na_rpb, w_o_diff, w_o_na, w_out):
    depth = w_in.shape[0]
    t5_tiles = _t5_tables(t5_rel_bias)
    f32 = jnp.float32
    for l in range(depth):
        lam_init = 0.8 - 0.6 * math.exp(-0.3 * l)
        lam = (jnp.exp(jnp.sum(lambda_q1[l].astype(f32) * lambda_k1[l].astype(f32)))
               - jnp.exp(jnp.sum(lambda_q2[l].astype(f32) * lambda_k2[l].astype(f32)))
               + lam_init).reshape(1)
        tables = (t5_tiles, _na_tables(na_rpb[l]))
        args = (pre_norm_w[l].reshape(1, D_MODEL).astype(f32),
                post_norm_w[l].reshape(1, D_MODEL).astype(f32),
                w_in[l].astype(jnp.bfloat16), lam,
                subln_w[l].reshape(1, DA_V_DIM).astype(f32),
                w_o_diff[l].astype(jnp.bfloat16), w_o_na[l].astype(jnp.bfloat16),
                w_out[l].astype(jnp.bfloat16), lam_init)
        xs = [_layer(x, tables, *args) for x in xs]
    return xs


def kernel(x_prompt, x_sample, t5_rel_bias, pre_norm_w, post_norm_w, w_in, lambda_q1, lambda_k1,
           lambda_q2, lambda_k2, subln_w, na_rpb, w_o_diff, w_o_na, w_out):
    y_prompt, y_sample = _trunk([x_prompt, x_sample], t5_rel_bias, pre_norm_w, post_norm_w, w_in,
                                lambda_q1, lambda_k1, lambda_q2, lambda_k2, subln_w, na_rpb,
                                w_o_diff, w_o_na, w_out)
    return (y_prompt, y_sample)
```

```python
import functools
import math

import jax
import jax.numpy as jnp
import numpy as np
from jax import lax
from jax.experimental import pallas as pl
from jax.experimental.pallas import tpu as pltpu

D_MODEL = 1024
DA_HEADS = 4
DA_HEAD_DIM = 64
DA_V_DIM = 2 * DA_HEAD_DIM
DA_WIDTH = DA_HEADS * DA_V_DIM
T5_BUCKETS = 32
T5_MAX_DIST = 128
GRID_W = 64
NA_HEADS = 8
NA_HEAD_DIM = 64
NA_WIDTH = NA_HEADS * NA_HEAD_DIM
NA_ROWS = 8
NA_COLS = 16
IN_WIDTH = 4 * DA_WIDTH + 4 * NA_WIDTH + 2 * D_MODEL
NORM_EPS = 1e-6
SUBLN_EPS = 1e-5
LOG2E = math.log2(math.e)

LANES = 128
VMEM_LIMIT_BYTES = 56 * 1024 * 1024

PROJ_TM = 512
PROJ_CHUNK = 512
DA_T = 256
DA_ROWS = 64
DA_BIAS_TILES = 5
NA_RB = 8
NA_GROUP = 4
OUT_TM = 512
NEG_BIG = -1e30

_QA, _KA, _VA = 0, DA_HEADS, 2 * DA_HEADS
_ZA_BLK, _QN_BLK, _KN_BLK, _VN_BLK, _ZN_BLK = 3, 4, 5, 6, 7
_G_BLK = 2

_NT = (((1,), (1,)), ((), ()))


def _params(sem):
    return pltpu.CompilerParams(dimension_semantics=sem, vmem_limit_bytes=VMEM_LIMIT_BYTES)


def _proj_kernel(x_ref, pw_ref, w_ref, o_ref, *, scales):
    x = x_ref[...]
    ms = jnp.mean(x * x, axis=-1, keepdims=True)
    h = (x * lax.rsqrt(ms + NORM_EPS) * pw_ref[...]).astype(jnp.bfloat16)
    for c, sc in enumerate(scales):
        cols = slice(c * PROJ_CHUNK, (c + 1) * PROJ_CHUNK)
        acc = jnp.dot(h, w_ref[:, cols], preferred_element_type=jnp.float32)
        if sc != 1.0:
            acc = acc * sc
        o_ref[:, cols] = acc.astype(jnp.bfloat16)


def _proj(x2d, pre_w, w_in_bf16):
    tokens = x2d.shape[0]
    assert tokens % PROJ_TM == 0
    qscale = LOG2E * DA_HEAD_DIM ** -0.5
    assert DA_HEAD_DIM == NA_HEAD_DIM
    scales = [1.0] * (IN_WIDTH // PROJ_CHUNK)
    scales[0] = qscale
    scales[(4 * DA_WIDTH) // PROJ_CHUNK] = qscale
    assert DA_WIDTH == PROJ_CHUNK and NA_WIDTH == PROJ_CHUNK
    return pl.pallas_call(
        functools.partial(_proj_kernel, scales=tuple(scales)),
        grid=(tokens // PROJ_TM,),
        in_specs=[
            pl.BlockSpec((PROJ_TM, D_MODEL), lambda i: (i, 0)),
            pl.BlockSpec((1, D_MODEL), lambda i: (0, 0)),
            pl.BlockSpec((D_MODEL, IN_WIDTH), lambda i: (0, 0)),
        ],
        out_specs=pl.BlockSpec((PROJ_TM, IN_WIDTH), lambda i: (i, 0)),
        out_shape=jax.ShapeDtypeStruct((tokens, IN_WIDTH), jnp.bfloat16),
        compiler_params=_params(("parallel",)),
        name="proj",
    )(x2d, pre_w, w_in_bf16)


def _lane_groups(x):
    return [x[:, g * LANES:(g + 1) * LANES] for g in range(x.shape[1] // LANES)]


def _da_kernel(lam_ref, q_ref, k_ref, v_ref, bias_ref, sw_ref, o_ref,
               s_scr, p_scr, q2_scr, l_scr, *, n_chunks, n_blocks, out_scale):
    T = DA_T
    R = DA_ROWS
    t = pl.program_id(0)

    @pl.when(t == 0)
    def _():
        s_scr[...] = jnp.zeros(s_scr.shape, jnp.float32)
        p_scr[...] = jnp.zeros(p_scr.shape, jnp.bfloat16)
        l_scr[...] = jnp.ones(l_scr.shape, jnp.float32)

    qi = lax.rem(jnp.minimum(t, n_blocks - 1), n_chunks)
    lane = lax.broadcasted_iota(jnp.int32, (T, LANES), 1)
    far = (DA_BIAS_TILES - 1) // 2

    def lane_tile(x):
        return jnp.concatenate([x] * (T // LANES), axis=1)

    def group_reduce(x, op):
        groups = _lane_groups(x)
        out = groups[0]
        for g in groups[1:]:
            out = op(out, g)
        return out

    def step(cur):
        prv = 1 - cur
        q = q_ref[0]
        zero = jnp.zeros_like(q)
        q2_scr[:T] = jnp.where(lane < DA_HEAD_DIM, q, zero)
        q2_scr[T:] = jnp.where(lane >= DA_HEAD_DIM, q, zero)
        l0 = jnp.sum(l_scr[cur, 0], axis=-1, keepdims=True)
        l1 = jnp.sum(l_scr[cur, 1], axis=-1, keepdims=True)
        ratio = lane_tile(jnp.broadcast_to(lam_ref[0] * l0 / l1, (T, LANES)).astype(jnp.bfloat16))

        def stage1(c):
            bias = bias_ref[0, jnp.clip(c - qi, -far, far) + far]
            s2 = lax.dot_general(q2_scr[...], k_ref[0, c * T:(c + 1) * T, :], _NT,
                                 preferred_element_type=jnp.float32)
            for m in range(2):
                s_scr[cur, m, c] = s2[m * T:(m + 1) * T] + bias

        def stage2(rb, m):
            rows = slice(rb * R, (rb + 1) * R)
            mx = group_reduce(s_scr[prv, m, 0, rows, :], jnp.maximum)
            for c in range(1, n_chunks):
                mx = jnp.maximum(mx, group_reduce(s_scr[prv, m, c, rows, :], jnp.maximum))
            mbt = lane_tile(jnp.broadcast_to(jnp.max(mx, axis=-1, keepdims=True), (R, LANES)))
            l = jnp.zeros((R, LANES), jnp.float32)
            for c in range(n_chunks):
                p = jnp.exp2(s_scr[prv, m, c, rows, :] - mbt)
                l = l + group_reduce(p, jnp.add)
                p_scr[prv, m, c, rows, :] = p.astype(jnp.bfloat16)
            l_scr[prv, m, rows, :] = l

        def stage3(c, acc):
            w = p_scr[cur, 0, c] - ratio * p_scr[cur, 1, c]
            return acc + jnp.dot(w, v_ref[0, c * T:(c + 1) * T, :],
                                 preferred_element_type=jnp.float32)

        acc = jnp.zeros((T, DA_V_DIM), jnp.float32)
        for c in range(n_chunks):
            stage1(c)
            acc = stage3(c, acc)
        for rb in range(T // R):
            for m in range(2):
                stage2(rb, m)

        o = acc / l0
        ms = jnp.mean(o * o, axis=-1, keepdims=True)
        o = o * lax.rsqrt(ms + SUBLN_EPS) * sw_ref[...] * out_scale
        o_ref[0] = o.astype(jnp.bfloat16)

    parity = lax.rem(t, 2)
    for cur in range(2):
        pl.when(parity == cur)(functools.partial(step, cur))


def _diff_attention(proj, lam, bias_tiles, subln_w, lam_init):
    B, S, _ = proj.shape
    T = DA_T
    assert S % T == 0 and T >= T5_MAX_DIST
    n = S // T
    n_blocks = B * DA_HEADS * n

    def block(t, lag):
        u = jnp.clip(t - lag, 0, n_blocks - 1)
        return u // (DA_HEADS * n), (u // n) % DA_HEADS, u % n

    def q_map(t):
        b, h, i = block(t, 0)
        return b, i, _QA + h

    def k_map(t):
        b, h, _ = block(t, 0)
        return b, 0, _KA + h

    def v_map(t):
        b, h, _ = block(t, 2)
        return b, 0, _VA + h

    def bias_map(t):
        _, h, _ = block(t, 0)
        return h, 0, 0, 0

    def o_map(t):
        b, h, i = block(t, 2)
        return b, i, h

    return pl.pallas_call(
        functools.partial(_da_kernel, n_chunks=n, n_blocks=n_blocks, out_scale=1.0 - lam_init),
        grid=(n_blocks + 2,),
        in_specs=[
            pl.BlockSpec(memory_space=pltpu.SMEM),
            pl.BlockSpec((1, T, LANES), q_map),
            pl.BlockSpec((1, S, LANES), k_map),
            pl.BlockSpec((1, S, LANES), v_map),
            pl.BlockSpec((1, DA_BIAS_TILES, T, T), bias_map),
            pl.BlockSpec((1, DA_V_DIM), lambda t: (0, 0)),
        ],
        out_specs=pl.BlockSpec((1, T, LANES), o_map),
        out_shape=jax.ShapeDtypeStruct((B, S, DA_WIDTH), jnp.bfloat16),
        scratch_shapes=[
            pltpu.VMEM((2, 2, n, T, T), jnp.float32),
            pltpu.VMEM((2, 2, n, T, T), jnp.bfloat16),
            pltpu.VMEM((2 * T, LANES), jnp.bfloat16),
            pltpu.VMEM((2, 2, T, LANES), jnp.float32),
        ],
        compiler_params=_params(("arbitrary",)),
        name="diff_attn",
    )(lam, proj, proj, proj, bias_tiles, subln_w)


def _na_kernel(q_ref, k_ref, v_ref, bias_ref, o_ref, *, rows):
    rb = pl.program_id(1)
    win = NA_ROWS * GRID_W
    gw = NA_GROUP * NA_HEAD_DIM
    head_of_lane = lax.broadcasted_iota(jnp.int32, (GRID_W, gw), 1) // NA_HEAD_DIM

    def row_body(i, carry):
        r = rb * NA_RB + i
        start = jnp.clip(r - NA_ROWS // 2, 0, rows - NA_ROWS)
        var = r - start
        krows = pl.ds(pl.multiple_of(start * GRID_W, GRID_W), win)
        qrows = pl.ds(pl.multiple_of(i * GRID_W, GRID_W), GRID_W)
        for g in range(NA_HEADS // NA_GROUP):
            cols = slice(g * gw, (g + 1) * gw)
            qg = q_ref[0, qrows, cols]
            zero = jnp.zeros_like(qg)
            qbd = jnp.concatenate([jnp.where(head_of_lane == h, qg, zero) for h in range(NA_GROUP)],
                                  axis=0)
            s = lax.dot_general(qbd, k_ref[0, krows, cols], _NT, preferred_element_type=jnp.float32)
            s = s + bias_ref[var, g]
            m = jnp.max(s, axis=-1, keepdims=True)
            p = jnp.exp2(s - m)
            l = jnp.sum(p, axis=-1, keepdims=True)
            pv = jnp.dot(p.astype(jnp.bfloat16), v_ref[0, krows, cols],
                         preferred_element_type=jnp.float32) / l
            o = pv[:GRID_W]
            for h in range(1, NA_GROUP):
                o = jnp.where(head_of_lane == h, pv[h * GRID_W:(h + 1) * GRID_W], o)
            o_ref[0, qrows, cols] = o.astype(jnp.bfloat16)
        return carry

    lax.fori_loop(0, NA_RB, row_body, 0, unroll=2)


def _neighbourhood_attention(proj, na_bias):
    B, S, _ = proj.shape
    rows = S // GRID_W
    assert S % GRID_W == 0 and rows >= NA_ROWS and rows % NA_RB == 0
    blk = NA_RB * GRID_W
    return pl.pallas_call(
        functools.partial(_na_kernel, rows=rows),
        grid=(B, rows // NA_RB),
        in_specs=[
            pl.BlockSpec((1, blk, NA_WIDTH), lambda b, i: (b, i, _QN_BLK)),
            pl.BlockSpec((1, S, NA_WIDTH), lambda b, i: (b, 0, _KN_BLK)),
            pl.BlockSpec((1, S, NA_WIDTH), lambda b, i: (b, 0, _VN_BLK)),
            pl.BlockSpec(na_bias.shape, lambda b, i: (0, 0, 0, 0)),
        ],
        out_specs=pl.BlockSpec((1, blk, NA_WIDTH), lambda b, i: (b, i, 0)),
        out_shape=jax.ShapeDtypeStruct((B, S, NA_WIDTH), jnp.bfloat16),
        compiler_params=_params(("parallel", "parallel")),
        name="nbr_attn",
    )(proj, proj, proj, na_bias)


def _out_kernel(x_ref, oa_ref, on_ref, za_ref, zn_ref, g_ref, woa_ref, won_ref, wout_ref,
                pw_ref, y_ref):
    def branch(o_ref, z_ref, w_ref):
        z = z_ref[...].astype(jnp.float32)
        u = o_ref[...].astype(jnp.float32) * (z * jax.nn.sigmoid(z))
        return jnp.dot(u.astype(jnp.bfloat16), w_ref[...], preferred_element_type=jnp.float32)

    ya = branch(oa_ref, za_ref, woa_ref)
    yn = branch(on_ref, zn_ref, won_ref)
    ga = jax.nn.sigmoid(g_ref[:, :D_MODEL].astype(jnp.float32))
    gn = jax.nn.sigmoid(g_ref[:, D_MODEL:].astype(jnp.float32))
    merged = (ga * ya + gn * yn).astype(jnp.bfloat16)
    out = jnp.dot(merged, wout_ref[...], preferred_element_type=jnp.float32)
    ms = jnp.mean(out * out, axis=-1, keepdims=True)
    y_ref[...] = x_ref[...] + out * lax.rsqrt(ms + NORM_EPS) * pw_ref[...]


def _output(x2d, oa2d, on2d, proj2d, w_o_diff, w_o_na, w_out, post_w):
    tokens = x2d.shape[0]
    tm = OUT_TM
    assert tokens % tm == 0

    def const(shape):
        return pl.BlockSpec(shape, lambda i: (0, 0))

    return pl.pallas_call(
        _out_kernel,
        grid=(tokens // tm,),
        in_specs=[
            pl.BlockSpec((tm, D_MODEL), lambda i: (i, 0)),
            pl.BlockSpec((tm, DA_WIDTH), lambda i: (i, 0)),
            pl.BlockSpec((tm, NA_WIDTH), lambda i: (i, 0)),
            pl.BlockSpec((tm, DA_WIDTH), lambda i: (i, _ZA_BLK)),
            pl.BlockSpec((tm, NA_WIDTH), lambda i: (i, _ZN_BLK)),
            pl.BlockSpec((tm, 2 * D_MODEL), lambda i: (i, _G_BLK)),
            const((DA_WIDTH, D_MODEL)),
            const((NA_WIDTH, D_MODEL)),
            const((D_MODEL, D_MODEL)),
            const((1, D_MODEL)),
        ],
        out_specs=pl.BlockSpec((tm, D_MODEL), lambda i: (i, 0)),
        out_shape=jax.ShapeDtypeStruct((tokens, D_MODEL), jnp.float32),
        compiler_params=_params(("parallel",)),
        name="out_proj",
    )(x2d, oa2d, on2d, proj2d, proj2d, proj2d, w_o_diff, w_o_na, w_out, post_w)


def _t5_bucket(rel):
    nb = T5_BUCKETS // 2
    ret = jnp.where(rel > 0, nb, 0)
    n = jnp.abs(rel)
    max_exact = nb // 2
    nf = jnp.maximum(n, 1).astype(jnp.float32)
    large = max_exact + (jnp.log(nf / max_exact) / math.log(T5_MAX_DIST / max_exact)
                         * (nb - max_exact)).astype(jnp.int32)
    large = jnp.minimum(large, nb - 1)
    return ret + jnp.where(n < max_exact, n, large)


def _toeplitz(w, rows, first, cols):
    M = w.shape[-1]
    assert first - (rows - 1) >= 0 and first + cols <= M - 1
    lead = w.shape[:-1]
    flat = jnp.tile(w, (1,) * len(lead) + (rows,))[..., :rows * (M - 1)]
    return flat.reshape(lead + (rows, M - 1))[..., first:first + cols]


def _t5_tables(t5_rel_bias):
    T = DA_T
    far = (DA_BIAS_TILES - 1) // 2
    M = (2 * far + 2) * T + 8
    rel = jnp.arange(M, dtype=jnp.int32) - (far + 1) * T
    vec = jnp.transpose(t5_rel_bias[_t5_bucket(rel)].astype(jnp.float32) * LOG2E)
    return jnp.stack([_toeplitz(vec, T, (d + far + 1) * T, T) for d in range(-far, far + 1)], axis=1)


def _na_tables(na_rpb):
    n_dr, n_dc = 2 * NA_ROWS - 1, 2 * NA_COLS - 1
    M = 2 * GRID_W
    pad = GRID_W - NA_COLS
    ring = jnp.pad(na_rpb.astype(jnp.float32) * LOG2E, ((0, 0), (0, 0), (pad, M - pad - n_dc)))
    band = _toeplitz(ring, GRID_W, GRID_W - 1, GRID_W)
    c = jnp.arange(GRID_W, dtype=jnp.int32)[:, None]
    kc = jnp.arange(GRID_W, dtype=jnp.int32)[None, :]
    cs = jnp.clip(c - NA_COLS // 2, 0, GRID_W - NA_COLS)
    band = jnp.where((kc >= cs) & (kc < cs + NA_COLS), band, NEG_BIG)
    per_var = [jnp.transpose(band[:, NA_ROWS - 1 - v:n_dr - v], (0, 2, 1, 3)) for v in range(NA_ROWS)]
    return jnp.stack(per_var, axis=0).reshape(NA_ROWS, NA_HEADS // NA_GROUP, NA_GROUP * GRID_W,
                                              NA_ROWS * GRID_W)


def _layer(x, tables, pre_w, post_w, w_in, lam, subln_w, w_o_diff, w_o_na, w_out, lam_init):
    B, S, D = x.shape
    t5_tiles, na_bias = tables
    x2d = x.reshape(B * S, D)
    proj2d = _proj(x2d, pre_w, w_in)
    proj = proj2d.reshape(B, S, IN_WIDTH)
    oa = _diff_attention(proj, lam, t5_tiles, subln_w, lam_init)
    on = _neighbourhood_attention(proj, na_bias)
    y = _output(x2d, oa.reshape(B * S, DA_WIDTH), on.reshape(B * S, NA_WIDTH), proj2d,
                w_o_diff, w_o_na, w_out, post_w)
    return y.reshape(B, S, D)


def _trunk(xs, t5_rel_bias, pre_norm_w, post_norm_w, w_in, lambda_q1, lambda_k1, lambda_q2,
           lambda_k2, subln_w, na_rpb, w_o_diff, w_o_na, w_out):
    depth = w_in.shape[0]
    t5_tiles = _t5_tables(t5_rel_bias)
    f32 = jnp.float32
    for l in range(depth):
        lam_init = 0.8 - 0.6 * math.exp(-0.3 * l)
        lam = (jnp.exp(jnp.sum(lambda_q1[l].astype(f32) * lambda_k1[l].astype(f32)))
               - jnp.exp(jnp.sum(lambda_q2[l].astype(f32) * lambda_k2[l].astype(f32)))
               + lam_init).reshape(1)
        tables = (t5_tiles, _na_tables(na_rpb[l]))
        args = (pre_norm_w[l].reshape(1, D_MODEL).astype(f32),
                post_norm_w[l].reshape(1, D_MODEL).astype(f32),
                w_in[l].astype(jnp.bfloat16), lam,
                subln_w[l].reshape(1, DA_V_DIM).astype(f32),
                w_o_diff[l].astype(jnp.bfloat16), w_o_na[l].astype(jnp.bfloat16),
                w_out[l].astype(jnp.bfloat16), lam_init)
        xs = [_layer(x, tables, *args) for x in xs]
    return xs


def kernel(x_prompt, x_sample, t5_rel_bias, pre_norm_w, post_norm_w, w_in, lambda_q1, lambda_k1,
           lambda_q2, lambda_k2, subln_w, na_rpb, w_o_diff, w_o_na, w_out):
    y_prompt, y_sample = _trunk([x_prompt, x_sample], t5_rel_bias, pre_norm_w, post_norm_w, w_in,
                                lambda_q1, lambda_k1, lambda_q2, lambda_k2, subln_w, na_rpb,
                                w_o_diff, w_o_na, w_out)
    return (y_prompt, y_sample)
```

```python
import functools
import math

import jax
import jax.numpy as jnp
import numpy as np
from jax import lax
from jax.experimental import pallas as pl
from jax.experimental.pallas import tpu as pltpu

D_MODEL = 1024
DA_HEADS = 4
DA_HEAD_DIM = 64
DA_V_DIM = 2 * DA_HEAD_DIM
DA_WIDTH = DA_HEADS * DA_V_DIM
T5_BUCKETS = 32
T5_MAX_DIST = 128
GRID_W = 64
NA_HEADS = 8
NA_HEAD_DIM = 64
NA_WIDTH = NA_HEADS * NA_HEAD_DIM
NA_ROWS = 8
NA_COLS = 16
IN_WIDTH = 4 * DA_WIDTH + 4 * NA_WIDTH + 2 * D_MODEL
NORM_EPS = 1e-6
SUBLN_EPS = 1e-5
LOG2E = math.log2(math.e)

LANES = 128
VMEM_LIMIT_BYTES = 56 * 1024 * 1024

PROJ_TM = 512
PROJ_CHUNK = 512
DA_T = 256
DA_ROWS = 64
DA_BIAS_TILES = 5
NA_RB = 8
NA_GROUP = 4
OUT_TM = 512
NEG_BIG = -1e30

_QA, _KA, _VA = 0, DA_HEADS, 2 * DA_HEADS
_ZA_BLK, _QN_BLK, _KN_BLK, _VN_BLK, _ZN_BLK = 3, 4, 5, 6, 7
_G_BLK = 2

_NT = (((1,), (1,)), ((), ()))


def _params(sem):
    return pltpu.CompilerParams(dimension_semantics=sem, vmem_limit_bytes=VMEM_LIMIT_BYTES)


def _proj_kernel(x_ref, pw_ref, w_ref, o_ref, *, scales):
    x = x_ref[...]
    ms = jnp.mean(x * x, axis=-1, keepdims=True)
    h = (x * lax.rsqrt(ms + NORM_EPS) * pw_ref[...]).astype(jnp.bfloat16)
    for c, sc in enumerate(scales):
        cols = slice(c * PROJ_CHUNK, (c + 1) * PROJ_CHUNK)
        acc = jnp.dot(h, w_ref[:, cols], preferred_element_type=jnp.float32)
        if sc != 1.0:
            acc = acc * sc
        o_ref[:, cols] = acc.astype(jnp.bfloat16)


def _proj(x2d, pre_w, w_in_bf16):
    tokens = x2d.shape[0]
    assert tokens % PROJ_TM == 0
    qscale = LOG2E * DA_HEAD_DIM ** -0.5
    assert DA_HEAD_DIM == NA_HEAD_DIM
    scales = [1.0] * (IN_WIDTH // PROJ_CHUNK)
    scales[0] = qscale
    scales[(4 * DA_WIDTH) // PROJ_CHUNK] = qscale
    assert DA_WIDTH == PROJ_CHUNK and NA_WIDTH == PROJ_CHUNK
    return pl.pallas_call(
        functools.partial(_proj_kernel, scales=tuple(scales)),
        grid=(tokens // PROJ_TM,),
        in_specs=[
            pl.BlockSpec((PROJ_TM, D_MODEL), lambda i: (i, 0)),
            pl.BlockSpec((1, D_MODEL), lambda i: (0, 0)),
            pl.BlockSpec((D_MODEL, IN_WIDTH), lambda i: (0, 0)),
        ],
        out_specs=pl.BlockSpec((PROJ_TM, IN_WIDTH), lambda i: (i, 0)),
        out_shape=jax.ShapeDtypeStruct((tokens, IN_WIDTH), jnp.bfloat16),
        compiler_params=_params(("parallel",)),
        name="proj",
    )(x2d, pre_w, w_in_bf16)


def _lane_groups(x):
    return [x[:, g * LANES:(g + 1) * LANES] for g in range(x.shape[1] // LANES)]


def _da_kernel(lam_ref, q_ref, k_ref, v_ref, bias_ref, sw_ref, o_ref,
               s_scr, q2_scr, mx_scr, m_scr, l_scr, *, n_chunks, n_blocks, out_scale):
    T = DA_T
    t = pl.program_id(0)

    @pl.when(t == 0)
    def _():
        s_scr[...] = jnp.zeros(s_scr.shape, jnp.float32)
        m_scr[...] = jnp.zeros(m_scr.shape, jnp.float32)

    qi = lax.rem(jnp.minimum(t, n_blocks - 1), n_chunks)
    lane = lax.broadcasted_iota(jnp.int32, (T, LANES), 1)
    far = (DA_BIAS_TILES - 1) // 2

    def lane_tile(x):
        return jnp.concatenate([x] * (T // LANES), axis=1)

    def group_reduce(x, op):
        groups = _lane_groups(x)
        out = groups[0]
        for g in groups[1:]:
            out = op(out, g)
        return out

    def step(cur):
        prv = 1 - cur
        q = q_ref[0]
        zero = jnp.zeros_like(q)
        q2_scr[:T] = jnp.where(lane < DA_HEAD_DIM, q, zero)
        q2_scr[T:] = jnp.where(lane >= DA_HEAD_DIM, q, zero)
        mx_scr[...] = jnp.full(mx_scr.shape, -jnp.inf, jnp.float32)
        l_scr[...] = jnp.zeros(l_scr.shape, jnp.float32)

        def stage1(c):
            bias = bias_ref[0, jnp.clip(c - qi, -far, far) + far]
            s2 = lax.dot_general(q2_scr[...], k_ref[0, c * T:(c + 1) * T, :], _NT,
                                 preferred_element_type=jnp.float32)
            for m in range(2):
                s = s2[m * T:(m + 1) * T] + bias
                s_scr[cur, m, c] = s
                mx_scr[m] = jnp.maximum(mx_scr[m], group_reduce(s, jnp.maximum))

        def stage2(c, acc):
            ps = []
            for m in range(2):
                p = jnp.exp2(s_scr[prv, m, c] - lane_tile(m_scr[prv, m]))
                l_scr[m] = l_scr[m] + group_reduce(p, jnp.add)
                ps.append(p.astype(jnp.bfloat16))
            return acc + jnp.dot(jnp.concatenate(ps, axis=0), v_ref[0, c * T:(c + 1) * T, :],
                                 preferred_element_type=jnp.float32)

        acc = jnp.zeros((2 * T, DA_V_DIM), jnp.float32)
        for c in range(n_chunks):
            stage1(c)
            acc = stage2(c, acc)
        for m in range(2):
            row_max = jnp.max(mx_scr[m], axis=-1, keepdims=True)
            m_scr[cur, m] = jnp.broadcast_to(row_max, (T, LANES))

        l0 = jnp.sum(l_scr[0], axis=-1, keepdims=True)
        l1 = jnp.sum(l_scr[1], axis=-1, keepdims=True)
        o = acc[:T] / l0 - lam_ref[0] * (acc[T:] / l1)
        ms = jnp.mean(o * o, axis=-1, keepdims=True)
        o = o * lax.rsqrt(ms + SUBLN_EPS) * sw_ref[...] * out_scale
        o_ref[0] = o.astype(jnp.bfloat16)

    parity = lax.rem(t, 2)
    for cur in range(2):
        pl.when(parity == cur)(functools.partial(step, cur))


def _diff_attention(proj, lam, bias_tiles, subln_w, lam_init):
    B, S, _ = proj.shape
    T = DA_T
    assert S % T == 0 and T >= T5_MAX_DIST
    n = S // T
    n_blocks = B * DA_HEADS * n

    def block(t, lag):
        u = jnp.clip(t - lag, 0, n_blocks - 1)
        return u // (DA_HEADS * n), (u // n) % DA_HEADS, u % n

    def q_map(t):
        b, h, i = block(t, 0)
        return b, i, _QA + h

    def k_map(t):
        b, h, _ = block(t, 0)
        return b, 0, _KA + h

    def v_map(t):
        b, h, _ = block(t, 1)
        return b, 0, _VA + h

    def bias_map(t):
        _, h, _ = block(t, 0)
        return h, 0, 0, 0

    def o_map(t):
        b, h, i = block(t, 1)
        return b, i, h

    return pl.pallas_call(
        functools.partial(_da_kernel, n_chunks=n, n_blocks=n_blocks, out_scale=1.0 - lam_init),
        grid=(n_blocks + 1,),
        in_specs=[
            pl.BlockSpec(memory_space=pltpu.SMEM),
            pl.BlockSpec((1, T, LANES), q_map),
            pl.BlockSpec((1, S, LANES), k_map),
            pl.BlockSpec((1, S, LANES), v_map),
            pl.BlockSpec((1, DA_BIAS_TILES, T, T), bias_map),
            pl.BlockSpec((1, DA_V_DIM), lambda t: (0, 0)),
        ],
        out_specs=pl.BlockSpec((1, T, LANES), o_map),
        out_shape=jax.ShapeDtypeStruct((B, S, DA_WIDTH), jnp.bfloat16),
        scratch_shapes=[
            pltpu.VMEM((2, 2, n, T, T), jnp.float32),
            pltpu.VMEM((2 * T, LANES), jnp.bfloat16),
            pltpu.VMEM((2, T, LANES), jnp.float32),
            pltpu.VMEM((2, 2, T, LANES), jnp.float32),
            pltpu.VMEM((2, T, LANES), jnp.float32),
        ],
        compiler_params=_params(("arbitrary",)),
        name="diff_attn",
    )(lam, proj, proj, proj, bias_tiles, subln_w)


def _na_kernel(q_ref, k_ref, v_ref, bias_ref, o_ref, *, rows):
    rb = pl.program_id(1)
    win = NA_ROWS * GRID_W
    gw = NA_GROUP * NA_HEAD_DIM
    head_of_lane = lax.broadcasted_iota(jnp.int32, (GRID_W, gw), 1) // NA_HEAD_DIM

    def row_body(i, carry):
        r = rb * NA_RB + i
        start = jnp.clip(r - NA_ROWS // 2, 0, rows - NA_ROWS)
        var = r - start
        krows = pl.ds(pl.multiple_of(start * GRID_W, GRID_W), win)
        qrows = pl.ds(pl.multiple_of(i * GRID_W, GRID_W), GRID_W)
        for g in range(NA_HEADS // NA_GROUP):
            cols = slice(g * gw, (g + 1) * gw)
            qg = q_ref[0, qrows, cols]
            zero = jnp.zeros_like(qg)
            qbd = jnp.concatenate([jnp.where(head_of_lane == h, qg, zero) for h in range(NA_GROUP)],
                                  axis=0)
            s = lax.dot_general(qbd, k_ref[0, krows, cols], _NT, preferred_element_type=jnp.float32)
            s = s + bias_ref[var, g]
            m = jnp.max(s, axis=-1, keepdims=True)
            p = jnp.exp2(s - m)
            l = jnp.sum(p, axis=-1, keepdims=True)
            pv = jnp.dot(p.astype(jnp.bfloat16), v_ref[0, krows, cols],
                         preferred_element_type=jnp.float32) / l
            o = pv[:GRID_W]
            for h in range(1, NA_GROUP):
                o = jnp.where(head_of_lane == h, pv[h * GRID_W:(h + 1) * GRID_W], o)
            o_ref[0, qrows, cols] = o.astype(jnp.bfloat16)
        return carry

    lax.fori_loop(0, NA_RB, row_body, 0, unroll=2)


def _neighbourhood_attention(proj, na_bias):
    B, S, _ = proj.shape
    rows = S // GRID_W
    assert S % GRID_W == 0 and rows >= NA_ROWS and rows % NA_RB == 0
    blk = NA_RB * GRID_W
    return pl.pallas_call(
        functools.partial(_na_kernel, rows=rows),
        grid=(B, rows // NA_RB),
        in_specs=[
            pl.BlockSpec((1, blk, NA_WIDTH), lambda b, i: (b, i, _QN_BLK)),
            pl.BlockSpec((1, S, NA_WIDTH), lambda b, i: (b, 0, _KN_BLK)),
            pl.BlockSpec((1, S, NA_WIDTH), lambda b, i: (b, 0, _VN_BLK)),
            pl.BlockSpec(na_bias.shape, lambda b, i: (0, 0, 0, 0)),
        ],
        out_specs=pl.BlockSpec((1, blk, NA_WIDTH), lambda b, i: (b, i, 0)),
        out_shape=jax.ShapeDtypeStruct((B, S, NA_WIDTH), jnp.bfloat16),
        compiler_params=_params(("parallel", "parallel")),
        name="nbr_attn",
    )(proj, proj, proj, na_bias)


def _out_kernel(x_ref, oa_ref, on_ref, za_ref, zn_ref, g_ref, woa_ref, won_ref, wout_ref,
                pw_ref, y_ref):
    def branch(o_ref, z_ref, w_ref):
        z = z_ref[...].astype(jnp.float32)
        u = o_ref[...].astype(jnp.float32) * (z * jax.nn.sigmoid(z))
        return jnp.dot(u.astype(jnp.bfloat16), w_ref[...], preferred_element_type=jnp.float32)

    ya = branch(oa_ref, za_ref, woa_ref)
    yn = branch(on_ref, zn_ref, won_ref)
    ga = jax.nn.sigmoid(g_ref[:, :D_MODEL].astype(jnp.float32))
    gn = jax.nn.sigmoid(g_ref[:, D_MODEL:].astype(jnp.float32))
    merged = (ga * ya + gn * yn).astype(jnp.bfloat16)
    out = jnp.dot(merged, wout_ref[...], preferred_element_type=jnp.float32)
    ms = jnp.mean(out * out, axis=-1, keepdims=True)
    y_ref[...] = x_ref[...] + out * lax.rsqrt(ms + NORM_EPS) * pw_ref[...]


def _output(x2d, oa2d, on2d, proj2d, w_o_diff, w_o_na, w_out, post_w):
    tokens = x2d.shape[0]
    tm = OUT_TM
    assert tokens % tm == 0

    def const(shape):
        return pl.BlockSpec(shape, lambda i: (0, 0))

    return pl.pallas_call(
        _out_kernel,
        grid=(tokens // tm,),
        in_specs=[
            pl.BlockSpec((tm, D_MODEL), lambda i: (i, 0)),
            pl.BlockSpec((tm, DA_WIDTH), lambda i: (i, 0)),
            pl.BlockSpec((tm, NA_WIDTH), lambda i: (i, 0)),
            pl.BlockSpec((tm, DA_WIDTH), lambda i: (i, _ZA_BLK)),
            pl.BlockSpec((tm, NA_WIDTH), lambda i: (i, _ZN_BLK)),
            pl.BlockSpec((tm, 2 * D_MODEL), lambda i: (i, _G_BLK)),
            const((DA_WIDTH, D_MODEL)),
            const((NA_WIDTH, D_MODEL)),
            const((D_MODEL, D_MODEL)),
            const((1, D_MODEL)),
        ],
        out_specs=pl.BlockSpec((tm, D_MODEL), lambda i: (i, 0)),
        out_shape=jax.ShapeDtypeStruct((tokens, D_MODEL), jnp.float32),
        compiler_params=_params(("parallel",)),
        name="out_proj",
    )(x2d, oa2d, on2d, proj2d, proj2d, proj2d, w_o_diff, w_o_na, w_out, post_w)


def _t5_bucket(rel):
    nb = T5_BUCKETS // 2
    ret = jnp.where(rel > 0, nb, 0)
    n = jnp.abs(rel)
    max_exact = nb // 2
    nf = jnp.maximum(n, 1).astype(jnp.float32)
    large = max_exact + (jnp.log(nf / max_exact) / math.log(T5_MAX_DIST / max_exact)
                         * (nb - max_exact)).astype(jnp.int32)
    large = jnp.minimum(large, nb - 1)
    return ret + jnp.where(n < max_exact, n, large)


def _toeplitz(w, rows, first, cols):
    M = w.shape[-1]
    assert first - (rows - 1) >= 0 and first + cols <= M - 1
    lead = w.shape[:-1]
    flat = jnp.tile(w, (1,) * len(lead) + (rows,))[..., :rows * (M - 1)]
    return flat.reshape(lead + (rows, M - 1))[..., first:first + cols]


def _t5_tables(t5_rel_bias):
    T = DA_T
    far = (DA_BIAS_TILES - 1) // 2
    M = (2 * far + 2) * T + 8
    rel = jnp.arange(M, dtype=jnp.int32) - (far + 1) * T
    vec = jnp.transpose(t5_rel_bias[_t5_bucket(rel)].astype(jnp.float32) * LOG2E)
    return jnp.stack([_toeplitz(vec, T, (d + far + 1) * T, T) for d in range(-far, far + 1)], axis=1)


def _na_tables(na_rpb):
    n_dr, n_dc = 2 * NA_ROWS - 1, 2 * NA_COLS - 1
    M = 2 * GRID_W
    pad = GRID_W - NA_COLS
    ring = jnp.pad(na_rpb.astype(jnp.float32) * LOG2E, ((0, 0), (0, 0), (pad, M - pad - n_dc)))
    band = _toeplitz(ring, GRID_W, GRID_W - 1, GRID_W)
    c = jnp.arange(GRID_W, dtype=jnp.int32)[:, None]
    kc = jnp.arange(GRID_W, dtype=jnp.int32)[None, :]
    cs = jnp.clip(c - NA_COLS // 2, 0, GRID_W - NA_COLS)
    band = jnp.where((kc >= cs) & (kc < cs + NA_COLS), band, NEG_BIG)
    per_var = [jnp.transpose(band[:, NA_ROWS - 1 - v:n_dr - v], (0, 2, 1, 3)) for v in range(NA_ROWS)]
    return jnp.stack(per_var, axis=0).reshape(NA_ROWS, NA_HEADS // NA_GROUP, NA_GROUP * GRID_W,
                                              NA_ROWS * GRID_W)


def _layer(x, tables, pre_w, post_w, w_in, lam, subln_w, w_o_diff, w_o_na, w_out, lam_init):
    B, S, D = x.shape
    t5_tiles, na_bias = tables
    x2d = x.reshape(B * S, D)
    proj2d = _proj(x2d, pre_w, w_in)
    proj = proj2d.reshape(B, S, IN_WIDTH)
    oa = _diff_attention(proj, lam, t5_tiles, subln_w, lam_init)
    on = _neighbourhood_attention(proj, na_bias)
    y = _output(x2d, oa.reshape(B * S, DA_WIDTH), on.reshape(B * S, NA_WIDTH), proj2d,
                w_o_diff, w_o_na, w_out, post_w)
    return y.reshape(B, S, D)


def _trunk(xs, t5_rel_bias, pre_norm_w, post_norm_w, w_in, lambda_q1, lambda_k1, lambda_q2,
           lambda_k2, subln_w, na_rpb, w_o_diff, w_o_na, w_out):
    depth = w_in.shape[0]
    t5_tiles = _t5_tables(t5_rel_bias)
    f32 = jnp.float32
    for l in range(depth):
        lam_init = 0.8 - 0.6 * math.exp(-0.3 * l)
        lam = (jnp.exp(jnp.sum(lambda_q1[l].astype(f32) * lambda_k1[l].astype(f32)))
               - jnp.exp(jnp.sum(lambda_q2[l].astype(f32) * lambda_k2[l].astype(f32)))
               + lam_init).reshape(1)
        tables = (t5_tiles, _na_tables(na_rpb[l]))
        args = (pre_norm_w[l].reshape(1, D_MODEL).astype(f32),
                post_norm_w[l].reshape(1, D_MODEL).astype(f32),
                w_in[l].astype(jnp.bfloat16), lam,
                subln_w[l].reshape(1, DA_V_DIM).astype(f32),
                w_o_diff[l].astype(jnp.bfloat16), w_o_na[l].astype(jnp.bfloat16),
                w_out[l].astype(jnp.bfloat16), lam_init)
        xs = [_layer(x, tables, *args) for x in xs]
    return xs


def kernel(x_prompt, x_sample, t5_rel_bias, pre_norm_w, post_norm_w, w_in, lambda_q1, lambda_k1,
           lambda_q2, lambda_k2, subln_w, na_rpb, w_o_diff, w_o_na, w_out):
    y_prompt, y_sample = _trunk([x_prompt, x_sample], t5_rel_bias, pre_norm_w, post_norm_w, w_in,
                                lambda_q1, lambda_k1, lambda_q2, lambda_k2, subln_w, na_rpb,
                                w_o_diff, w_o_na, w_out)
    return (y_prompt, y_sample)
```

```python
import functools
import math

import jax
import jax.numpy as jnp
import numpy as np
from jax import lax
from jax.experimental import pallas as pl
from jax.experimental.pallas import tpu as pltpu

D_MODEL = 1024
DA_HEADS = 4
DA_HEAD_DIM = 64
DA_V_DIM = 2 * DA_HEAD_DIM
DA_WIDTH = DA_HEADS * DA_V_DIM
T5_BUCKETS = 32
T5_MAX_DIST = 128
GRID_W = 64
NA_HEADS = 8
NA_HEAD_DIM = 64
NA_WIDTH = NA_HEADS * NA_HEAD_DIM
NA_ROWS = 8
NA_COLS = 16
IN_WIDTH = 4 * DA_WIDTH + 4 * NA_WIDTH + 2 * D_MODEL
NORM_EPS = 1e-6
SUBLN_EPS = 1e-5
LOG2E = math.log2(math.e)

LANES = 128
SUBLANES = 8
VMEM_LIMIT_BYTES = 56 * 1024 * 1024

PROJ_TM = 512
PROJ_CHUNK = 512
DA_T = 256
DA_VT_ROWS = DA_V_DIM + 16
DA_BIAS_TILES = 5
NA_RB = 8
NA_GROUP = 4
OUT_TM = 512
NEG_BIG = -1e30

_QA, _KA, _VA = 0, DA_HEADS, 2 * DA_HEADS
_ZA_BLK, _QN_BLK, _KN_BLK, _VN_BLK, _ZN_BLK = 3, 4, 5, 6, 7
_G_BLK = 2

_NT = (((1,), (1,)), ((), ()))


def _params(sem):
    return pltpu.CompilerParams(dimension_semantics=sem, vmem_limit_bytes=VMEM_LIMIT_BYTES)


def _proj_kernel(x_ref, pw_ref, w_ref, o_ref, *, scales):
    x = x_ref[...]
    ms = jnp.mean(x * x, axis=-1, keepdims=True)
    h = (x * lax.rsqrt(ms + NORM_EPS) * pw_ref[...]).astype(jnp.bfloat16)
    for c, sc in enumerate(scales):
        cols = slice(c * PROJ_CHUNK, (c + 1) * PROJ_CHUNK)
        acc = jnp.dot(h, w_ref[:, cols], preferred_element_type=jnp.float32)
        if sc != 1.0:
            acc = acc * sc
        o_ref[:, cols] = acc.astype(jnp.bfloat16)


def _proj(x2d, pre_w, w_in_bf16):
    tokens = x2d.shape[0]
    assert tokens % PROJ_TM == 0
    qscale = LOG2E * DA_HEAD_DIM ** -0.5
    assert DA_HEAD_DIM == NA_HEAD_DIM
    scales = [1.0] * (IN_WIDTH // PROJ_CHUNK)
    scales[0] = qscale
    scales[(4 * DA_WIDTH) // PROJ_CHUNK] = qscale
    assert DA_WIDTH == PROJ_CHUNK and NA_WIDTH == PROJ_CHUNK
    return pl.pallas_call(
        functools.partial(_proj_kernel, scales=tuple(scales)),
        grid=(tokens // PROJ_TM,),
        in_specs=[
            pl.BlockSpec((PROJ_TM, D_MODEL), lambda i: (i, 0)),
            pl.BlockSpec((1, D_MODEL), lambda i: (0, 0)),
            pl.BlockSpec((D_MODEL, IN_WIDTH), lambda i: (0, 0)),
        ],
        out_specs=pl.BlockSpec((PROJ_TM, IN_WIDTH), lambda i: (i, 0)),
        out_shape=jax.ShapeDtypeStruct((tokens, IN_WIDTH), jnp.bfloat16),
        compiler_params=_params(("parallel",)),
        name="proj",
    )(x2d, pre_w, w_in_bf16)


def _sublane_groups_reduce(x, op):
    parts = [x[g * SUBLANES:(g + 1) * SUBLANES] for g in range(x.shape[0] // SUBLANES)]
    while len(parts) > 1:
        parts = [op(a, b) for a, b in zip(parts[0::2], parts[1::2])] + parts[len(parts) & ~1:]
    return parts[0]


def _da_kernel(lam_ref, q_ref, k_ref, vt_ref, bias_ref, sw_ref, o_ref,
               s_scr, q2_scr, m_scr, acc_scr, *, n_chunks, n_blocks, out_scale):
    T = DA_T
    t = pl.program_id(0)

    @pl.when(t == 0)
    def _():
        s_scr[...] = jnp.zeros(s_scr.shape, jnp.float32)
        m_scr[...] = jnp.zeros(m_scr.shape, jnp.float32)
        acc_scr[...] = jnp.ones(acc_scr.shape, jnp.float32)

    qi = lax.rem(jnp.minimum(t, n_blocks - 1), n_chunks)
    lane = lax.broadcasted_iota(jnp.int32, (T, LANES), 1)
    far = (DA_BIAS_TILES - 1) // 2

    def sublane_tile(x):
        return jnp.concatenate([x] * (T // SUBLANES), axis=0)

    def step(cur):
        prv = 1 - cur
        q = q_ref[0]
        zero = jnp.zeros_like(q)
        q2_scr[:T] = jnp.where(lane < DA_HEAD_DIM, q, zero)
        q2_scr[T:] = jnp.where(lane >= DA_HEAD_DIM, q, zero)
        m_prev = sublane_tile(m_scr[prv])

        done = acc_scr[...]
        l = done[DA_V_DIM:DA_V_DIM + 1]
        ot = done[:DA_V_DIM, :T] / l[:, :T] - lam_ref[0] * (done[:DA_V_DIM, T:] / l[:, T:])
        o = ot.T
        ms = jnp.mean(o * o, axis=-1, keepdims=True)
        o = o * lax.rsqrt(ms + SUBLN_EPS) * sw_ref[...] * out_scale
        o_ref[0] = o.astype(jnp.bfloat16)

        def stage1(c, mx):
            bias = bias_ref[0, jnp.clip(c - qi, -far, far) + far]
            st = lax.dot_general(k_ref[0, c * T:(c + 1) * T, :], q2_scr[...], _NT,
                                 preferred_element_type=jnp.float32)
            st = st + jnp.concatenate([bias, bias], axis=1)
            s_scr[cur, c] = st
            return jnp.maximum(mx, _sublane_groups_reduce(st, jnp.maximum))

        def stage2(c, acc):
            pt = jnp.exp2(s_scr[prv, c] - m_prev)
            return acc + jnp.dot(vt_ref[0, 0, :, c * T:(c + 1) * T], pt.astype(jnp.bfloat16),
                                 preferred_element_type=jnp.float32)

        acc = jnp.zeros((DA_VT_ROWS, 2 * T), jnp.float32)
        mx = jnp.full((SUBLANES, 2 * T), -jnp.inf, jnp.float32)
        for c in range(n_chunks):
            mx = stage1(c, mx)
            acc = stage2(c, acc)
        m_scr[cur] = jnp.broadcast_to(jnp.max(mx, axis=0, keepdims=True), (SUBLANES, 2 * T))
        acc_scr[...] = acc

    parity = lax.rem(t, 2)
    for cur in range(2):
        pl.when(parity == cur)(functools.partial(step, cur))


def _diff_attention(proj, lam, bias_tiles, subln_w, lam_init):
    B, S, _ = proj.shape
    T = DA_T
    assert S % T == 0 and T >= T5_MAX_DIST
    n = S // T
    n_blocks = B * DA_HEADS * n

    def block(t, lag):
        u = jnp.clip(t - lag, 0, n_blocks - 1)
        return u // (DA_HEADS * n), (u // n) % DA_HEADS, u % n

    def q_map(t):
        b, h, i = block(t, 0)
        return b, i, _QA + h

    def k_map(t):
        b, h, _ = block(t, 0)
        return b, 0, _KA + h

    def v_map(t):
        b, h, _ = block(t, 1)
        return b, h, 0, 0

    vt = jnp.transpose(proj[:, :, _VA * LANES:(_VA + DA_HEADS) * LANES]
                       .reshape(B, S, DA_HEADS, DA_V_DIM), (0, 2, 3, 1))
    vt = jnp.concatenate([vt, jnp.ones((B, DA_HEADS, DA_VT_ROWS - DA_V_DIM, S), vt.dtype)], axis=2)

    def bias_map(t):
        _, h, _ = block(t, 0)
        return h, 0, 0, 0

    def o_map(t):
        b, h, i = block(t, 2)
        return b, i, h

    return pl.pallas_call(
        functools.partial(_da_kernel, n_chunks=n, n_blocks=n_blocks, out_scale=1.0 - lam_init),
        grid=(n_blocks + 2,),
        in_specs=[
            pl.BlockSpec(memory_space=pltpu.SMEM),
            pl.BlockSpec((1, T, LANES), q_map),
            pl.BlockSpec((1, S, LANES), k_map),
            pl.BlockSpec((1, 1, DA_VT_ROWS, S), v_map),
            pl.BlockSpec((1, DA_BIAS_TILES, T, T), bias_map),
            pl.BlockSpec((1, DA_V_DIM), lambda t: (0, 0)),
        ],
        out_specs=pl.BlockSpec((1, T, LANES), o_map),
        out_shape=jax.ShapeDtypeStruct((B, S, DA_WIDTH), jnp.bfloat16),
        scratch_shapes=[
            pltpu.VMEM((2, n, T, 2 * T), jnp.float32),
            pltpu.VMEM((2 * T, LANES), jnp.bfloat16),
            pltpu.VMEM((2, SUBLANES, 2 * T), jnp.float32),
            pltpu.VMEM((DA_VT_ROWS, 2 * T), jnp.float32),
        ],
        compiler_params=_params(("arbitrary",)),
        name="diff_attn",
    )(lam, proj, proj, vt, jnp.swapaxes(bias_tiles, -1, -2), subln_w)


def _na_kernel(q_ref, k_ref, v_ref, bias_ref, o_ref, *, rows):
    rb = pl.program_id(1)
    win = NA_ROWS * GRID_W
    gw = NA_GROUP * NA_HEAD_DIM
    head_of_lane = lax.broadcasted_iota(jnp.int32, (GRID_W, gw), 1) // NA_HEAD_DIM

    def row_body(i, carry):
        r = rb * NA_RB + i
        start = jnp.clip(r - NA_ROWS // 2, 0, rows - NA_ROWS)
        var = r - start
        krows = pl.ds(pl.multiple_of(start * GRID_W, GRID_W), win)
        qrows = pl.ds(pl.multiple_of(i * GRID_W, GRID_W), GRID_W)
        for g in range(NA_HEADS // NA_GROUP):
            cols = slice(g * gw, (g + 1) * gw)
            qg = q_ref[0, qrows, cols]
            zero = jnp.zeros_like(qg)
            qbd = jnp.concatenate([jnp.where(head_of_lane == h, qg, zero) for h in range(NA_GROUP)],
                                  axis=0)
            s = lax.dot_general(qbd, k_ref[0, krows, cols], _NT, preferred_element_type=jnp.float32)
            s = s + bias_ref[var, g]
            m = jnp.max(s, axis=-1, keepdims=True)
            p = jnp.exp2(s - m)
            l = jnp.sum(p, axis=-1, keepdims=True)
            pv = jnp.dot(p.astype(jnp.bfloat16), v_ref[0, krows, cols],
                         preferred_element_type=jnp.float32) / l
            o = pv[:GRID_W]
            for h in range(1, NA_GROUP):
                o = jnp.where(head_of_lane == h, pv[h * GRID_W:(h + 1) * GRID_W], o)
            o_ref[0, qrows, cols] = o.astype(jnp.bfloat16)
        return carry

    lax.fori_loop(0, NA_RB, row_body, 0, unroll=True)


def _neighbourhood_attention(proj, na_bias):
    B, S, _ = proj.shape
    rows = S // GRID_W
    assert S % GRID_W == 0 and rows >= NA_ROWS and rows % NA_RB == 0
    blk = NA_RB * GRID_W
    return pl.pallas_call(
        functools.partial(_na_kernel, rows=rows),
        grid=(B, rows // NA_RB),
        in_specs=[
            pl.BlockSpec((1, blk, NA_WIDTH), lambda b, i: (b, i, _QN_BLK)),
            pl.BlockSpec((1, S, NA_WIDTH), lambda b, i: (b, 0, _KN_BLK)),
            pl.BlockSpec((1, S, NA_WIDTH), lambda b, i: (b, 0, _VN_BLK)),
            pl.BlockSpec(na_bias.shape, lambda b, i: (0, 0, 0, 0)),
        ],
        out_specs=pl.BlockSpec((1, blk, NA_WIDTH), lambda b, i: (b, i, 0)),
        out_shape=jax.ShapeDtypeStruct((B, S, NA_WIDTH), jnp.bfloat16),
        compiler_params=_params(("parallel", "parallel")),
        name="nbr_attn",
    )(proj, proj, proj, na_bias)


def _out_kernel(x_ref, oa_ref, on_ref, za_ref, zn_ref, g_ref, woa_ref, won_ref, wout_ref,
                pw_ref, y_ref):
    def branch(o_ref, z_ref, w_ref):
        z = z_ref[...].astype(jnp.float32)
        u = o_ref[...].astype(jnp.float32) * (z * jax.nn.sigmoid(z))
        return jnp.dot(u.astype(jnp.bfloat16), w_ref[...], preferred_element_type=jnp.float32)

    ya = branch(oa_ref, za_ref, woa_ref)
    yn = branch(on_ref, zn_ref, won_ref)
    ga = jax.nn.sigmoid(g_ref[:, :D_MODEL].astype(jnp.float32))
    gn = jax.nn.sigmoid(g_ref[:, D_MODEL:].astype(jnp.float32))
    merged = (ga * ya + gn * yn).astype(jnp.bfloat16)
    out = jnp.dot(merged, wout_ref[...], preferred_element_type=jnp.float32)
    ms = jnp.mean(out * out, axis=-1, keepdims=True)
    y_ref[...] = x_ref[...] + out * lax.rsqrt(ms + NORM_EPS) * pw_ref[...]


def _output(x2d, oa2d, on2d, proj2d, w_o_diff, w_o_na, w_out, post_w):
    tokens = x2d.shape[0]
    tm = OUT_TM
    assert tokens % tm == 0

    def const(shape):
        return pl.BlockSpec(shape, lambda i: (0, 0))

    return pl.pallas_call(
        _out_kernel,
        grid=(tokens // tm,),
        in_specs=[
            pl.BlockSpec((tm, D_MODEL), lambda i: (i, 0)),
            pl.BlockSpec((tm, DA_WIDTH), lambda i: (i, 0)),
            pl.BlockSpec((tm, NA_WIDTH), lambda i: (i, 0)),
            pl.BlockSpec((tm, DA_WIDTH), lambda i: (i, _ZA_BLK)),
            pl.BlockSpec((tm, NA_WIDTH), lambda i: (i, _ZN_BLK)),
            pl.BlockSpec((tm, 2 * D_MODEL), lambda i: (i, _G_BLK)),
            const((DA_WIDTH, D_MODEL)),
            const((NA_WIDTH, D_MODEL)),
            const((D_MODEL, D_MODEL)),
            const((1, D_MODEL)),
        ],
        out_specs=pl.BlockSpec((tm, D_MODEL), lambda i: (i, 0)),
        out_shape=jax.ShapeDtypeStruct((tokens, D_MODEL), jnp.float32),
        compiler_params=_params(("parallel",)),
        name="out_proj",
    )(x2d, oa2d, on2d, proj2d, proj2d, proj2d, w_o_diff, w_o_na, w_out, post_w)


def _t5_bucket(rel):
    nb = T5_BUCKETS // 2
    ret = jnp.where(rel > 0, nb, 0)
    n = jnp.abs(rel)
    max_exact = nb // 2
    nf = jnp.maximum(n, 1).astype(jnp.float32)
    large = max_exact + (jnp.log(nf / max_exact) / math.log(T5_MAX_DIST / max_exact)
                         * (nb - max_exact)).astype(jnp.int32)
    large = jnp.minimum(large, nb - 1)
    return ret + jnp.where(n < max_exact, n, large)


def _toeplitz(w, rows, first, cols):
    M = w.shape[-1]
    assert first - (rows - 1) >= 0 and first + cols <= M - 1
    lead = w.shape[:-1]
    flat = jnp.tile(w, (1,) * len(lead) + (rows,))[..., :rows * (M - 1)]
    return flat.reshape(lead + (rows, M - 1))[..., first:first + cols]


def _t5_tables(t5_rel_bias):
    T = DA_T
    far = (DA_BIAS_TILES - 1) // 2
    M = (2 * far + 2) * T + 8
    rel = jnp.arange(M, dtype=jnp.int32) - (far + 1) * T
    vec = jnp.transpose(t5_rel_bias[_t5_bucket(rel)].astype(jnp.float32) * LOG2E)
    return jnp.stack([_toeplitz(vec, T, (d + far + 1) * T, T) for d in range(-far, far + 1)], axis=1)


def _na_tables(na_rpb):
    n_dr, n_dc = 2 * NA_ROWS - 1, 2 * NA_COLS - 1
    M = 2 * GRID_W
    pad = GRID_W - NA_COLS
    ring = jnp.pad(na_rpb.astype(jnp.float32) * LOG2E, ((0, 0), (0, 0), (pad, M - pad - n_dc)))
    band = _toeplitz(ring, GRID_W, GRID_W - 1, GRID_W)
    c = jnp.arange(GRID_W, dtype=jnp.int32)[:, None]
    kc = jnp.arange(GRID_W, dtype=jnp.int32)[None, :]
    cs = jnp.clip(c - NA_COLS // 2, 0, GRID_W - NA_COLS)
    band = jnp.where((kc >= cs) & (kc < cs + NA_COLS), band, NEG_BIG)
    per_var = [jnp.transpose(band[:, NA_ROWS - 1 - v:n_dr - v], (0, 2, 1, 3)) for v in range(NA_ROWS)]
    return jnp.stack(per_var, axis=0).reshape(NA_ROWS, NA_HEADS // NA_GROUP, NA_GROUP * GRID_W,
                                              NA_ROWS * GRID_W)


def _layer(x, tables, pre_w, post_w, w_in, lam, subln_w, w_o_diff, w_o_na, w_out, lam_init):
    B, S, D = x.shape
    t5_tiles, na_bias = tables
    x2d = x.reshape(B * S, D)
    proj2d = _proj(x2d, pre_w, w_in)
    proj = proj2d.reshape(B, S, IN_WIDTH)
    oa = _diff_attention(proj, lam, t5_tiles, subln_w, lam_init)
    on = _neighbourhood_attention(proj, na_bias)
    y = _output(x2d, oa.reshape(B * S, DA_WIDTH), on.reshape(B * S, NA_WIDTH), proj2d,
                w_o_diff, w_o_na, w_out, post_w)
    return y.reshape(B, S, D)


def _trunk(xs, t5_rel_bias, pre_norm_w, post_norm_w, w_in, lambda_q1, lambda_k1, lambda_q2,
           lambda_k2, subln_w, na_rpb, w_o_diff, w_o_na, w_out):
    depth = w_in.shape[0]
    t5_tiles = _t5_tables(t5_rel_bias)
    f32 = jnp.float32
    for l in range(depth):
        lam_init = 0.8 - 0.6 * math.exp(-0.3 * l)
        lam = (jnp.exp(jnp.sum(lambda_q1[l].astype(f32) * lambda_k1[l].astype(f32)))
               - jnp.exp(jnp.sum(lambda_q2[l].astype(f32) * lambda_k2[l].astype(f32)))
               + lam_init).reshape(1)
        tables = (t5_tiles, _na_tables(na_rpb[l]))
        args = (pre_norm_w[l].reshape(1, D_MODEL).astype(f32),
                post_norm_w[l].reshape(1, D_MODEL).astype(f32),
                w_in[l].astype(jnp.bfloat16), lam,
                subln_w[l].reshape(1, DA_V_DIM).astype(f32),
                w_o_diff[l].astype(jnp.bfloat16), w_o_na[l].astype(jnp.bfloat16),
                w_out[l].astype(jnp.bfloat16), lam_init)
        xs = [_layer(x, tables, *args) for x in xs]
    return xs


def kernel(x_prompt, x_sample, t5_rel_bias, pre_norm_w, post_norm_w, w_in, lambda_q1, lambda_k1,
           lambda_q2, lambda_k2, subln_w, na_rpb, w_o_diff, w_o_na, w_out):
    y_prompt, y_sample = _trunk([x_prompt, x_sample], t5_rel_bias, pre_norm_w, post_norm_w, w_in,
                                lambda_q1, lambda_k1, lambda_q2, lambda_k2, subln_w, na_rpb,
                                w_o_diff, w_o_na, w_out)
    return (y_prompt, y_sample)
```

```python
import functools
import math

import jax
import jax.numpy as jnp
from jax import lax
from jax.experimental import pallas as pl
from jax.experimental.pallas import tpu as pltpu

D_MODEL = 1024
DA_HEADS = 4
DA_HEAD_DIM = 64
DA_V_DIM = 2 * DA_HEAD_DIM
DA_WIDTH = DA_HEADS * DA_V_DIM
T5_BUCKETS = 32
T5_MAX_DIST = 128
GRID_W = 64
NA_HEADS = 8
NA_HEAD_DIM = 64
NA_WIDTH = NA_HEADS * NA_HEAD_DIM
NA_ROWS = 8
NA_COLS = 16
IN_WIDTH = 4 * DA_WIDTH + 4 * NA_WIDTH + 2 * D_MODEL
NORM_EPS = 1e-6
SUBLN_EPS = 1e-5
LOG2E = math.log2(math.e)

LANES = 128
VMEM_LIMIT_BYTES = 56 * 1024 * 1024

PROJ_TM = 512
PROJ_CHUNK = 512
DA_TQ = 512
DA_TK = 256
DA_D_LO = -((T5_MAX_DIST + DA_TK - 1 + DA_TK - 1) // DA_TK)
DA_D_HI = (T5_MAX_DIST + DA_TQ - 1 + DA_TK - 1) // DA_TK
DA_BIAS_TILES = DA_D_HI - DA_D_LO + 1
NA_RB = 8
NA_GROUP = 4
OUT_TM = 512
NEG_BIG = -1e30

_QA, _KA, _VA = 0, DA_HEADS, 2 * DA_HEADS
_ZA_BLK, _QN_BLK, _KN_BLK, _VN_BLK, _ZN_BLK = 3, 4, 5, 6, 7
_G_BLK = 2

_NT = (((1,), (1,)), ((), ()))


def _params(sem):
    return pltpu.CompilerParams(dimension_semantics=sem, vmem_limit_bytes=VMEM_LIMIT_BYTES)


def _proj_kernel(x_ref, pw_ref, w_ref, o_ref, *, scales):
    x = x_ref[...]
    ms = jnp.mean(x * x, axis=-1, keepdims=True)
    h = (x * lax.rsqrt(ms + NORM_EPS) * pw_ref[...]).astype(jnp.bfloat16)
    for c, sc in enumerate(scales):
        cols = slice(c * PROJ_CHUNK, (c + 1) * PROJ_CHUNK)
        acc = jnp.dot(h, w_ref[:, cols], preferred_element_type=jnp.float32)
        if sc != 1.0:
            acc = acc * sc
        o_ref[:, cols] = acc.astype(jnp.bfloat16)


def _proj(x2d, pre_w, w_in_bf16):
    tokens = x2d.shape[0]
    assert tokens % PROJ_TM == 0
    qscale = LOG2E * DA_HEAD_DIM ** -0.5
    assert DA_HEAD_DIM == NA_HEAD_DIM
    scales = [1.0] * (IN_WIDTH // PROJ_CHUNK)
    scales[0] = qscale
    scales[(4 * DA_WIDTH) // PROJ_CHUNK] = qscale
    assert DA_WIDTH == PROJ_CHUNK and NA_WIDTH == PROJ_CHUNK
    return pl.pallas_call(
        functools.partial(_proj_kernel, scales=tuple(scales)),
        grid=(tokens // PROJ_TM,),
        in_specs=[
            pl.BlockSpec((PROJ_TM, D_MODEL), lambda i: (i, 0)),
            pl.BlockSpec((1, D_MODEL), lambda i: (0, 0)),
            pl.BlockSpec((D_MODEL, IN_WIDTH), lambda i: (0, 0)),
        ],
        out_specs=pl.BlockSpec((PROJ_TM, IN_WIDTH), lambda i: (i, 0)),
        out_shape=jax.ShapeDtypeStruct((tokens, IN_WIDTH), jnp.bfloat16),
        compiler_params=_params(("parallel",)),
        name="proj",
    )(x2d, pre_w, w_in_bf16)


def _lane_groups(x):
    return [x[:, g * LANES:(g + 1) * LANES] for g in range(x.shape[1] // LANES)]


def _da_kernel(lam_ref, q_ref, k_ref, v_ref, bias_ref, sw_ref, o_ref,
               s_scr, q2_scr, mx_scr, m_scr, l_scr, acc_scr, *, n_chunks, n_blocks, out_scale):
    T = DA_TQ
    TK = DA_TK
    t = pl.program_id(0)

    @pl.when(t == 0)
    def _():
        s_scr[...] = jnp.zeros(s_scr.shape, jnp.float32)
        m_scr[...] = jnp.zeros(m_scr.shape, jnp.float32)
        l_scr[...] = jnp.ones(l_scr.shape, jnp.float32)
        acc_scr[...] = jnp.zeros(acc_scr.shape, jnp.float32)

    c0 = lax.rem(jnp.minimum(t, n_blocks - 1), n_chunks * TK // T) * (T // TK)
    lane = lax.broadcasted_iota(jnp.int32, (T, LANES), 1)

    def lane_tile(x):
        return jnp.concatenate([x] * (TK // LANES), axis=1)

    def group_reduce(x, op):
        groups = _lane_groups(x)
        out = groups[0]
        for g in groups[1:]:
            out = op(out, g)
        return out

    def step(cur):
        prv = 1 - cur
        q = q_ref[0]
        zero = jnp.zeros_like(q)
        q2_scr[:T] = jnp.where(lane < DA_HEAD_DIM, q, zero)
        q2_scr[T:] = jnp.where(lane >= DA_HEAD_DIM, q, zero)
        mx_scr[...] = jnp.full(mx_scr.shape, -jnp.inf, jnp.float32)

        l0 = jnp.sum(l_scr[cur, 0], axis=-1, keepdims=True)
        l1 = jnp.sum(l_scr[cur, 1], axis=-1, keepdims=True)
        o = acc_scr[:T] / l0 - lam_ref[0] * (acc_scr[T:] / l1)
        ms = jnp.mean(o * o, axis=-1, keepdims=True)
        o = o * lax.rsqrt(ms + SUBLN_EPS) * sw_ref[...] * out_scale
        o_ref[0] = o.astype(jnp.bfloat16)
        l_scr[prv] = jnp.zeros((2, T, LANES), jnp.float32)

        def stage1(c):
            bias = bias_ref[0, jnp.clip(c - c0, DA_D_LO, DA_D_HI) - DA_D_LO]
            s2 = lax.dot_general(q2_scr[...], k_ref[0, c * TK:(c + 1) * TK, :], _NT,
                                 preferred_element_type=jnp.float32)
            for m in range(2):
                s = s2[m * T:(m + 1) * T] + bias
                s_scr[cur, m, c] = s
                mx_scr[m] = jnp.maximum(mx_scr[m], group_reduce(s, jnp.maximum))

        def stage2(c, acc):
            ps = []
            for m in range(2):
                p = jnp.exp2(s_scr[prv, m, c] - lane_tile(m_scr[prv, m]))
                l_scr[prv, m] = l_scr[prv, m] + group_reduce(p, jnp.add)
                ps.append(p.astype(jnp.bfloat16))
            return acc + jnp.dot(jnp.concatenate(ps, axis=0), v_ref[0, c * TK:(c + 1) * TK, :],
                                 preferred_element_type=jnp.float32)

        acc = jnp.zeros((2 * T, DA_V_DIM), jnp.float32)
        for c in range(n_chunks):
            stage1(c)
            acc = stage2(c, acc)
        for m in range(2):
            row_max = jnp.max(mx_scr[m], axis=-1, keepdims=True)
            m_scr[cur, m] = jnp.broadcast_to(row_max, (T, LANES))
        acc_scr[...] = acc

    parity = lax.rem(t, 2)
    for cur in range(2):
        pl.when(parity == cur)(functools.partial(step, cur))


def _diff_attention(proj, lam, bias_tiles, subln_w, lam_init):
    B, S, _ = proj.shape
    T, TK = DA_TQ, DA_TK
    assert S % T == 0 and T % TK == 0
    n = S // TK
    nq = S // T
    n_blocks = B * DA_HEADS * nq

    def block(t, lag):
        u = jnp.clip(t - lag, 0, n_blocks - 1)
        return u // (DA_HEADS * nq), (u // nq) % DA_HEADS, u % nq

    def q_map(t):
        b, h, i = block(t, 0)
        return b, i, _QA + h

    def k_map(t):
        b, h, _ = block(t, 0)
        return b, 0, _KA + h

    def v_map(t):
        b, h, _ = block(t, 1)
        return b, 0, _VA + h

    def bias_map(t):
        _, h, _ = block(t, 0)
        return h, 0, 0, 0

    def o_map(t):
        b, h, i = block(t, 2)
        return b, i, h

    return pl.pallas_call(
        functools.partial(_da_kernel, n_chunks=n, n_blocks=n_blocks, out_scale=1.0 - lam_init),
        grid=(n_blocks + 2,),
        in_specs=[
            pl.BlockSpec(memory_space=pltpu.SMEM),
            pl.BlockSpec((1, T, LANES), q_map),
            pl.BlockSpec((1, S, LANES), k_map),
            pl.BlockSpec((1, S, LANES), v_map),
            pl.BlockSpec((1, DA_BIAS_TILES, T, TK), bias_map),
            pl.BlockSpec((1, DA_V_DIM), lambda t: (0, 0)),
        ],
        out_specs=pl.BlockSpec((1, T, LANES), o_map),
        out_shape=jax.ShapeDtypeStruct((B, S, DA_WIDTH), jnp.bfloat16),
        scratch_shapes=[
            pltpu.VMEM((2, 2, n, T, TK), jnp.float32),
            pltpu.VMEM((2 * T, LANES), jnp.bfloat16),
            pltpu.VMEM((2, T, LANES), jnp.float32),
            pltpu.VMEM((2, 2, T, LANES), jnp.float32),
            pltpu.VMEM((2, 2, T, LANES), jnp.float32),
            pltpu.VMEM((2 * T, DA_V_DIM), jnp.float32),
        ],
        compiler_params=_params(("arbitrary",)),
        name="diff_attn",
    )(lam, proj, proj, proj, bias_tiles, subln_w)


def _na_kernel(q_ref, k_ref, v_ref, bias_ref, o_ref, *, rows):
    rb = pl.program_id(1)
    win = NA_ROWS * GRID_W
    gw = NA_GROUP * NA_HEAD_DIM
    head_of_lane = lax.broadcasted_iota(jnp.int32, (GRID_W, gw), 1) // NA_HEAD_DIM

    def row_body(i, carry):
        r = rb * NA_RB + i
        start = jnp.clip(r - NA_ROWS // 2, 0, rows - NA_ROWS)
        var = r - start
        krows = pl.ds(pl.multiple_of(start * GRID_W, GRID_W), win)
        qrows = pl.ds(pl.multiple_of(i * GRID_W, GRID_W), GRID_W)
        for g in range(NA_HEADS // NA_GROUP):
            cols = slice(g * gw, (g + 1) * gw)
            qg = q_ref[0, qrows, cols]
            zero = jnp.zeros_like(qg)
            qbd = jnp.concatenate([jnp.where(head_of_lane == h, qg, zero) for h in range(NA_GROUP)],
                                  axis=0)
            s = lax.dot_general(qbd, k_ref[0, krows, cols], _NT, preferred_element_type=jnp.float32)
            s = s + bias_ref[var, g]
            m = jnp.max(s, axis=-1, keepdims=True)
            p = jnp.exp2(s - m)
            l = jnp.sum(p, axis=-1, keepdims=True)
            pv = jnp.dot(p.astype(jnp.bfloat16), v_ref[0, krows, cols],
                         preferred_element_type=jnp.float32) / l
            o = pv[:GRID_W]
            for h in range(1, NA_GROUP):
                o = jnp.where(head_of_lane == h, pv[h * GRID_W:(h + 1) * GRID_W], o)
            o_ref[0, qrows, cols] = o.astype(jnp.bfloat16)
        return carry

    lax.fori_loop(0, NA_RB, row_body, 0, unroll=True)


def _neighbourhood_attention(proj, na_bias):
    B, S, _ = proj.shape
    rows = S // GRID_W
    assert S % GRID_W == 0 and rows >= NA_ROWS and rows % NA_RB == 0
    blk = NA_RB * GRID_W
    return pl.pallas_call(
        functools.partial(_na_kernel, rows=rows),
        grid=(B, rows // NA_RB),
        in_specs=[
            pl.BlockSpec((1, blk, NA_WIDTH), lambda b, i: (b, i, _QN_BLK)),
            pl.BlockSpec((1, S, NA_WIDTH), lambda b, i: (b, 0, _KN_BLK)),
            pl.BlockSpec((1, S, NA_WIDTH), lambda b, i: (b, 0, _VN_BLK)),
            pl.BlockSpec(na_bias.shape, lambda b, i: (0, 0, 0, 0)),
        ],
        out_specs=pl.BlockSpec((1, blk, NA_WIDTH), lambda b, i: (b, i, 0)),
        out_shape=jax.ShapeDtypeStruct((B, S, NA_WIDTH), jnp.bfloat16),
        compiler_params=_params(("parallel", "parallel")),
        name="nbr_attn",
    )(proj, proj, proj, na_bias)


def _out_kernel(x_ref, oa_ref, on_ref, za_ref, zn_ref, g_ref, woa_ref, won_ref, wout_ref,
                pw_ref, y_ref):
    def branch(o_ref, z_ref, w_ref):
        z = z_ref[...].astype(jnp.float32)
        u = o_ref[...].astype(jnp.float32) * (z * jax.nn.sigmoid(z))
        return jnp.dot(u.astype(jnp.bfloat16), w_ref[...], preferred_element_type=jnp.float32)

    ya = branch(oa_ref, za_ref, woa_ref)
    yn = branch(on_ref, zn_ref, won_ref)
    ga = jax.nn.sigmoid(g_ref[:, :D_MODEL].astype(jnp.float32))
    gn = jax.nn.sigmoid(g_ref[:, D_MODEL:].astype(jnp.float32))
    merged = (ga * ya + gn * yn).astype(jnp.bfloat16)
    out = jnp.dot(merged, wout_ref[...], preferred_element_type=jnp.float32)
    ms = jnp.mean(out * out, axis=-1, keepdims=True)
    y_ref[...] = x_ref[...] + out * lax.rsqrt(ms + NORM_EPS) * pw_ref[...]


def _output(x2d, oa2d, on2d, proj2d, w_o_diff, w_o_na, w_out, post_w):
    tokens = x2d.shape[0]
    tm = OUT_TM
    assert tokens % tm == 0

    def const(shape):
        return pl.BlockSpec(shape, lambda i: (0, 0))

    return pl.pallas_call(
        _out_kernel,
        grid=(tokens // tm,),
        in_specs=[
            pl.BlockSpec((tm, D_MODEL), lambda i: (i, 0)),
            pl.BlockSpec((tm, DA_WIDTH), lambda i: (i, 0)),
            pl.BlockSpec((tm, NA_WIDTH), lambda i: (i, 0)),
            pl.BlockSpec((tm, DA_WIDTH), lambda i: (i, _ZA_BLK)),
            pl.BlockSpec((tm, NA_WIDTH), lambda i: (i, _ZN_BLK)),
            pl.BlockSpec((tm, 2 * D_MODEL), lambda i: (i, _G_BLK)),
            const((DA_WIDTH, D_MODEL)),
            const((NA_WIDTH, D_MODEL)),
            const((D_MODEL, D_MODEL)),
            const((1, D_MODEL)),
        ],
        out_specs=pl.BlockSpec((tm, D_MODEL), lambda i: (i, 0)),
        out_shape=jax.ShapeDtypeStruct((tokens, D_MODEL), jnp.float32),
        compiler_params=_params(("parallel",)),
        name="out_proj",
    )(x2d, oa2d, on2d, proj2d, proj2d, proj2d, w_o_diff, w_o_na, w_out, post_w)


def _t5_bucket(rel):
    nb = T5_BUCKETS // 2
    ret = jnp.where(rel > 0, nb, 0)
    n = jnp.abs(rel)
    max_exact = nb // 2
    nf = jnp.maximum(n, 1).astype(jnp.float32)
    large = max_exact + (jnp.log(nf / max_exact) / math.log(T5_MAX_DIST / max_exact)
                         * (nb - max_exact)).astype(jnp.int32)
    large = jnp.minimum(large, nb - 1)
    return ret + jnp.where(n < max_exact, n, large)


def _toeplitz(w, rows, first, cols):
    M = w.shape[-1]
    assert first - (rows - 1) >= 0 and first + cols <= M - 1
    lead = w.shape[:-1]
    flat = jnp.tile(w, (1,) * len(lead) + (rows,))[..., :rows * (M - 1)]
    return flat.reshape(lead + (rows, M - 1))[..., first:first + cols]


def _t5_tables(t5_rel_bias):
    T, TK = DA_TQ, DA_TK
    base = T - 1 - DA_D_LO * TK
    M = (DA_D_HI - DA_D_LO + 1) * TK + T + 8
    rel = jnp.arange(M, dtype=jnp.int32) - base
    vec = jnp.transpose(t5_rel_bias[_t5_bucket(rel)].astype(jnp.float32) * LOG2E)
    return jnp.stack([_toeplitz(vec, T, d * TK + base, TK) for d in range(DA_D_LO, DA_D_HI + 1)],
                     axis=1)


def _na_tables(na_rpb):
    n_dr, n_dc = 2 * NA_ROWS - 1, 2 * NA_COLS - 1
    M = 2 * GRID_W
    pad = GRID_W - NA_COLS
    ring = jnp.pad(na_rpb.astype(jnp.float32) * LOG2E, ((0, 0), (0, 0), (pad, M - pad - n_dc)))
    band = _toeplitz(ring, GRID_W, GRID_W - 1, GRID_W)
    c = jnp.arange(GRID_W, dtype=jnp.int32)[:, None]
    kc = jnp.arange(GRID_W, dtype=jnp.int32)[None, :]
    cs = jnp.clip(c - NA_COLS // 2, 0, GRID_W - NA_COLS)
    band = jnp.where((kc >= cs) & (kc < cs + NA_COLS), band, NEG_BIG)
    per_var = [jnp.transpose(band[:, NA_ROWS - 1 - v:n_dr - v], (0, 2, 1, 3)) for v in range(NA_ROWS)]
    return jnp.stack(per_var, axis=0).reshape(NA_ROWS, NA_HEADS // NA_GROUP, NA_GROUP * GRID_W,
                                              NA_ROWS * GRID_W)


def _layer(x, tables, pre_w, post_w, w_in, lam, subln_w, w_o_diff, w_o_na, w_out, lam_init):
    B, S, D = x.shape
    t5_tiles, na_bias = tables
    x2d = x.reshape(B * S, D)
    proj2d = _proj(x2d, pre_w, w_in)
    proj = proj2d.reshape(B, S, IN_WIDTH)
    oa = _diff_attention(proj, lam, t5_tiles, subln_w, lam_init)
    on = _neighbourhood_attention(proj, na_bias)
    y = _output(x2d, oa.reshape(B * S, DA_WIDTH), on.reshape(B * S, NA_WIDTH), proj2d,
                w_o_diff, w_o_na, w_out, post_w)
    return y.reshape(B, S, D)


def _trunk(xs, t5_rel_bias, pre_norm_w, post_norm_w, w_in, lambda_q1, lambda_k1, lambda_q2,
           lambda_k2, subln_w, na_rpb, w_o_diff, w_o_na, w_out):
    depth = w_in.shape[0]
    t5_tiles = _t5_tables(t5_rel_bias)
    f32 = jnp.float32
    for l in range(depth):
        lam_init = 0.8 - 0.6 * math.exp(-0.3 * l)
        lam = (jnp.exp(jnp.sum(lambda_q1[l].astype(f32) * lambda_k1[l].astype(f32)))
               - jnp.exp(jnp.sum(lambda_q2[l].astype(f32) * lambda_k2[l].astype(f32)))
               + lam_init).reshape(1)
        tables = (t5_tiles, _na_tables(na_rpb[l]))
        args = (pre_norm_w[l].reshape(1, D_MODEL).astype(f32),
                post_norm_w[l].reshape(1, D_MODEL).astype(f32),
                w_in[l].astype(jnp.bfloat16), lam,
                subln_w[l].reshape(1, DA_V_DIM).astype(f32),
                w_o_diff[l].astype(jnp.bfloat16), w_o_na[l].astype(jnp.bfloat16),
                w_out[l].astype(jnp.bfloat16), lam_init)
        xs = [_layer(x, tables, *args) for x in xs]
    return xs


def kernel(x_prompt, x_sample, t5_rel_bias, pre_norm_w, post_norm_w, w_in, lambda_q1, lambda_k1,
           lambda_q2, lambda_k2, subln_w, na_rpb, w_o_diff, w_o_na, w_out):
    y_prompt, y_sample = _trunk([x_prompt, x_sample], t5_rel_bias, pre_norm_w, post_norm_w, w_in,
                                lambda_q1, lambda_k1, lambda_q2, lambda_k2, subln_w, na_rpb,
                                w_o_diff, w_o_na, w_out)
    return (y_prompt, y_sample)
```

```python
import functools
import math

import jax
import jax.numpy as jnp
from jax import lax
from jax.experimental import pallas as pl
from jax.experimental.pallas import tpu as pltpu

D_MODEL = 1024
DA_HEADS = 4
DA_HEAD_DIM = 64
DA_V_DIM = 2 * DA_HEAD_DIM
DA_WIDTH = DA_HEADS * DA_V_DIM
T5_BUCKETS = 32
T5_MAX_DIST = 128
GRID_W = 64
NA_HEADS = 8
NA_HEAD_DIM = 64
NA_WIDTH = NA_HEADS * NA_HEAD_DIM
NA_ROWS = 8
NA_COLS = 16
IN_WIDTH = 4 * DA_WIDTH + 4 * NA_WIDTH + 2 * D_MODEL
NORM_EPS = 1e-6
SUBLN_EPS = 1e-5
LOG2E = math.log2(math.e)

LANES = 128
VMEM_LIMIT_BYTES = 56 * 1024 * 1024

PROJ_TM = 512
PROJ_CHUNK = 512
DA_TQ = 512
DA_TK = 256
DA_D_LO = -((T5_MAX_DIST + DA_TK - 1 + DA_TK - 1) // DA_TK)
DA_D_HI = (T5_MAX_DIST + DA_TQ - 1 + DA_TK - 1) // DA_TK
DA_BIAS_TILES = DA_D_HI - DA_D_LO + 1
NA_RB = 8
NA_GROUP = 4
OUT_TM = 512
NEG_BIG = -1e30

_QA, _KA, _VA = 0, DA_HEADS, 2 * DA_HEADS
_ZA_BLK, _QN_BLK, _KN_BLK, _VN_BLK, _ZN_BLK = 3, 4, 5, 6, 7
_G_BLK = 2

_NT = (((1,), (1,)), ((), ()))


def _params(sem):
    return pltpu.CompilerParams(dimension_semantics=sem, vmem_limit_bytes=VMEM_LIMIT_BYTES)


def _proj_kernel(x_ref, pw_ref, w_ref, o_ref, *, scales):
    x = x_ref[...]
    ms = jnp.mean(x * x, axis=-1, keepdims=True)
    h = (x * lax.rsqrt(ms + NORM_EPS) * pw_ref[...]).astype(jnp.bfloat16)
    for c, sc in enumerate(scales):
        cols = slice(c * PROJ_CHUNK, (c + 1) * PROJ_CHUNK)
        acc = jnp.dot(h, w_ref[:, cols], preferred_element_type=jnp.float32)
        if sc != 1.0:
            acc = acc * sc
        o_ref[:, cols] = acc.astype(jnp.bfloat16)


def _proj(x2d, pre_w, w_in_bf16):
    tokens = x2d.shape[0]
    assert tokens % PROJ_TM == 0
    qscale = LOG2E * DA_HEAD_DIM ** -0.5
    assert DA_HEAD_DIM == NA_HEAD_DIM
    scales = [1.0] * (IN_WIDTH // PROJ_CHUNK)
    scales[0] = qscale
    scales[(4 * DA_WIDTH) // PROJ_CHUNK] = qscale
    assert DA_WIDTH == PROJ_CHUNK and NA_WIDTH == PROJ_CHUNK
    return pl.pallas_call(
        functools.partial(_proj_kernel, scales=tuple(scales)),
        grid=(tokens // PROJ_TM,),
        in_specs=[
            pl.BlockSpec((PROJ_TM, D_MODEL), lambda i: (i, 0)),
            pl.BlockSpec((1, D_MODEL), lambda i: (0, 0)),
            pl.BlockSpec((D_MODEL, IN_WIDTH), lambda i: (0, 0)),
        ],
        out_specs=pl.BlockSpec((PROJ_TM, IN_WIDTH), lambda i: (i, 0)),
        out_shape=jax.ShapeDtypeStruct((tokens, IN_WIDTH), jnp.bfloat16),
        compiler_params=_params(("parallel",)),
        name="proj",
    )(x2d, pre_w, w_in_bf16)


def _lane_groups(x):
    return [x[:, g * LANES:(g + 1) * LANES] for g in range(x.shape[1] // LANES)]


def _da_kernel(lam_ref, q_ref, k_ref, v_ref, bias_ref, sw_ref, o_ref,
               s_scr, q2_scr, mx_scr, m_scr, l_scr, acc_scr, *, n_chunks, n_blocks, out_scale):
    T = DA_TQ
    TK = DA_TK
    t = pl.program_id(0)

    @pl.when(t == 0)
    def _():
        s_scr[...] = jnp.zeros(s_scr.shape, jnp.float32)
        m_scr[...] = jnp.zeros(m_scr.shape, jnp.float32)
        l_scr[...] = jnp.ones(l_scr.shape, jnp.float32)
        acc_scr[...] = jnp.zeros(acc_scr.shape, jnp.float32)

    c0 = lax.rem(jnp.minimum(t, n_blocks - 1), n_chunks * TK // T) * (T // TK)
    lane = lax.broadcasted_iota(jnp.int32, (T, LANES), 1)

    def lane_tile(x):
        return jnp.concatenate([x] * (TK // LANES), axis=1)

    def group_reduce(x, op):
        groups = _lane_groups(x)
        out = groups[0]
        for g in groups[1:]:
            out = op(out, g)
        return out

    def step(cur):
        prv = 1 - cur
        q = q_ref[0]
        zero = jnp.zeros_like(q)
        q2_scr[:T] = jnp.where(lane < DA_HEAD_DIM, q, zero)
        q2_scr[T:] = jnp.where(lane >= DA_HEAD_DIM, q, zero)
        mx_scr[...] = jnp.full(mx_scr.shape, -jnp.inf, jnp.float32)

        l0 = jnp.sum(l_scr[cur, 0], axis=-1, keepdims=True)
        l1 = jnp.sum(l_scr[cur, 1], axis=-1, keepdims=True)
        o = acc_scr[:T] / l0 - lam_ref[0] * (acc_scr[T:] / l1)
        ms = jnp.mean(o * o, axis=-1, keepdims=True)
        o = o * lax.rsqrt(ms + SUBLN_EPS) * sw_ref[...] * out_scale
        o_ref[0] = o.astype(jnp.bfloat16)
        l_scr[prv] = jnp.zeros((2, T, LANES), jnp.float32)

        def stage1(c):
            bias = bias_ref[0, jnp.clip(c - c0, DA_D_LO, DA_D_HI) - DA_D_LO]
            s2 = lax.dot_general(q2_scr[...], k_ref[0, c * TK:(c + 1) * TK, :], _NT,
                                 preferred_element_type=jnp.float32)
            for m in range(2):
                s = s2[m * T:(m + 1) * T] + bias
                s_scr[cur, m, c] = s
                mx_scr[m] = jnp.maximum(mx_scr[m], group_reduce(s, jnp.maximum))

        def stage2(c, acc):
            ps = []
            for m in range(2):
                p = jnp.exp2(s_scr[prv, m, c] - lane_tile(m_scr[prv, m]))
                l_scr[prv, m] = l_scr[prv, m] + group_reduce(p, jnp.add)
                ps.append(p.astype(jnp.bfloat16))
            return acc + jnp.dot(jnp.concatenate(ps, axis=0), v_ref[0, c * TK:(c + 1) * TK, :],
                                 preferred_element_type=jnp.float32)

        acc = jnp.zeros((2 * T, DA_V_DIM), jnp.float32)
        for c in range(n_chunks):
            stage1(c)
            acc = stage2(c, acc)
        for m in range(2):
            row_max = jnp.max(mx_scr[m], axis=-1, keepdims=True)
            m_scr[cur, m] = jnp.broadcast_to(row_max, (T, LANES))
        acc_scr[...] = acc

    parity = lax.rem(t, 2)
    for cur in range(2):
        pl.when(parity == cur)(functools.partial(step, cur))


def _diff_attention(proj, lam, bias_tiles, subln_w, lam_init):
    B, S, _ = proj.shape
    T, TK = DA_TQ, DA_TK
    assert S % T == 0 and T % TK == 0
    n = S // TK
    nq = S // T
    n_blocks = B * DA_HEADS * nq

    def block(t, lag):
        u = jnp.clip(t - lag, 0, n_blocks - 1)
        return u // (DA_HEADS * nq), (u // nq) % DA_HEADS, u % nq

    def q_map(t):
        b, h, i = block(t, 0)
        return b, i, _QA + h

    def k_map(t):
        b, h, _ = block(t, 0)
        return b, 0, _KA + h

    def v_map(t):
        b, h, _ = block(t, 1)
        return b, 0, _VA + h

    def bias_map(t):
        _, h, _ = block(t, 0)
        return h, 0, 0, 0

    def o_map(t):
        b, h, i = block(t, 2)
        return b, i, h

    return pl.pallas_call(
        functools.partial(_da_kernel, n_chunks=n, n_blocks=n_blocks, out_scale=1.0 - lam_init),
        grid=(n_blocks + 2,),
        in_specs=[
            pl.BlockSpec(memory_space=pltpu.SMEM),
            pl.BlockSpec((1, T, LANES), q_map),
            pl.BlockSpec((1, S, LANES), k_map),
            pl.BlockSpec((1, S, LANES), v_map),
            pl.BlockSpec((1, DA_BIAS_TILES, T, TK), bias_map),
            pl.BlockSpec((1, DA_V_DIM), lambda t: (0, 0)),
        ],
        out_specs=pl.BlockSpec((1, T, LANES), o_map),
        out_shape=jax.ShapeDtypeStruct((B, S, DA_WIDTH), jnp.bfloat16),
        scratch_shapes=[
            pltpu.VMEM((2, 2, n, T, TK), jnp.float32),
            pltpu.VMEM((2 * T, LANES), jnp.bfloat16),
            pltpu.VMEM((2, T, LANES), jnp.float32),
            pltpu.VMEM((2, 2, T, LANES), jnp.float32),
            pltpu.VMEM((2, 2, T, LANES), jnp.float32),
            pltpu.VMEM((2 * T, DA_V_DIM), jnp.float32),
        ],
        compiler_params=_params(("arbitrary",)),
        name="diff_attn",
    )(lam, proj, proj, proj, bias_tiles, subln_w)


def _na_kernel(q_ref, k_ref, v_ref, bias_ref, o_ref, *, rows):
    rb = pl.program_id(1)
    win = NA_ROWS * GRID_W
    gw = NA_GROUP * NA_HEAD_DIM
    head_of_lane = lax.broadcasted_iota(jnp.int32, (GRID_W, gw), 1) // NA_HEAD_DIM

    def row_body(i, carry):
        r = rb * NA_RB + i
        start = jnp.clip(r - NA_ROWS // 2, 0, rows - NA_ROWS)
        var = r - start
        krows = pl.ds(pl.multiple_of(start * GRID_W, GRID_W), win)
        qrows = pl.ds(pl.multiple_of(i * GRID_W, GRID_W), GRID_W)
        for g in range(NA_HEADS // NA_GROUP):
            cols = slice(g * gw, (g + 1) * gw)
            qg = q_ref[0, qrows, cols]
            zero = jnp.zeros_like(qg)
            qbd = jnp.concatenate([jnp.where(head_of_lane == h, qg, zero) for h in range(NA_GROUP)],
                                  axis=0)
            s = lax.dot_general(qbd, k_ref[0, krows, cols], _NT, preferred_element_type=jnp.float32)
            s = s + bias_ref[var, g]
            m = jnp.max(s, axis=-1, keepdims=True)
            p = jnp.exp2(s - m)
            l = jnp.sum(p, axis=-1, keepdims=True)
            pv = jnp.dot(p.astype(jnp.bfloat16), v_ref[0, krows, cols],
                         preferred_element_type=jnp.float32) / l
            o = pv[:GRID_W]
            for h in range(1, NA_GROUP):
                o = jnp.where(head_of_lane == h, pv[h * GRID_W:(h + 1) * GRID_W], o)
            o_ref[0, qrows, cols] = o.astype(jnp.bfloat16)
        return carry

    lax.fori_loop(0, NA_RB, row_body, 0, unroll=True)


def _neighbourhood_attention(proj, na_bias):
    B, S, _ = proj.shape
    rows = S // GRID_W
    assert S % GRID_W == 0 and rows >= NA_ROWS and rows % NA_RB == 0
    blk = NA_RB * GRID_W
    return pl.pallas_call(
        functools.partial(_na_kernel, rows=rows),
        grid=(B, rows // NA_RB),
        in_specs=[
            pl.BlockSpec((1, blk, NA_WIDTH), lambda b, i: (b, i, _QN_BLK)),
            pl.BlockSpec((1, S, NA_WIDTH), lambda b, i: (b, 0, _KN_BLK)),
            pl.BlockSpec((1, S, NA_WIDTH), lambda b, i: (b, 0, _VN_BLK)),
            pl.BlockSpec(na_bias.shape, lambda b, i: (0, 0, 0, 0)),
        ],
        out_specs=pl.BlockSpec((1, blk, NA_WIDTH), lambda b, i: (b, i, 0)),
        out_shape=jax.ShapeDtypeStruct((B, S, NA_WIDTH), jnp.bfloat16),
        compiler_params=_params(("parallel", "parallel")),
        name="nbr_attn",
    )(proj, proj, proj, na_bias)


def _out_kernel(x_ref, oa_ref, on_ref, za_ref, zn_ref, g_ref, woa_ref, won_ref, wout_ref,
                pw_ref, y_ref):
    def branch(o_ref, z_ref, w_ref):
        z = z_ref[...].astype(jnp.float32)
        u = o_ref[...].astype(jnp.float32) * (z * jax.nn.sigmoid(z))
        return jnp.dot(u.astype(jnp.bfloat16), w_ref[...], preferred_element_type=jnp.float32)

    ya = branch(oa_ref, za_ref, woa_ref)
    yn = branch(on_ref, zn_ref, won_ref)
    ga = jax.nn.sigmoid(g_ref[:, :D_MODEL].astype(jnp.float32))
    gn = jax.nn.sigmoid(g_ref[:, D_MODEL:].astype(jnp.float32))
    merged = (ga * ya + gn * yn).astype(jnp.bfloat16)
    out = jnp.dot(merged, wout_ref[...], preferred_element_type=jnp.float32)
    ms = jnp.mean(out * out, axis=-1, keepdims=True)
    y_ref[...] = x_ref[...] + out * lax.rsqrt(ms + NORM_EPS) * pw_ref[...]


def _output(x2d, oa2d, on2d, proj2d, w_o_diff, w_o_na, w_out, post_w):
    tokens = x2d.shape[0]
    tm = OUT_TM
    assert tokens % tm == 0

    def const(shape):
        return pl.BlockSpec(shape, lambda i: (0, 0))

    return pl.pallas_call(
        _out_kernel,
        grid=(tokens // tm,),
        in_specs=[
            pl.BlockSpec((tm, D_MODEL), lambda i: (i, 0)),
            pl.BlockSpec((tm, DA_WIDTH), lambda i: (i, 0)),
            pl.BlockSpec((tm, NA_WIDTH), lambda i: (i, 0)),
            pl.BlockSpec((tm, DA_WIDTH), lambda i: (i, _ZA_BLK)),
            pl.BlockSpec((tm, NA_WIDTH), lambda i: (i, _ZN_BLK)),
            pl.BlockSpec((tm, 2 * D_MODEL), lambda i: (i, _G_BLK)),
            const((DA_WIDTH, D_MODEL)),
            const((NA_WIDTH, D_MODEL)),
            const((D_MODEL, D_MODEL)),
            const((1, D_MODEL)),
        ],
        out_specs=pl.BlockSpec((tm, D_MODEL), lambda i: (i, 0)),
        out_shape=jax.ShapeDtypeStruct((tokens, D_MODEL), jnp.float32),
        compiler_params=_params(("parallel",)),
        name="out_proj",
    )(x2d, oa2d, on2d, proj2d, proj2d, proj2d, w_o_diff, w_o_na, w_out, post_w)


def _t5_bucket(rel):
    nb = T5_BUCKETS // 2
    ret = jnp.where(rel > 0, nb, 0)
    n = jnp.abs(rel)
    max_exact = nb // 2
    nf = jnp.maximum(n, 1).astype(jnp.float32)
    large = max_exact + (jnp.log(nf / max_exact) / math.log(T5_MAX_DIST / max_exact)
                         * (nb - max_exact)).astype(jnp.int32)
    large = jnp.minimum(large, nb - 1)
    return ret + jnp.where(n < max_exact, n, large)


def _toeplitz(w, rows, first, cols):
    M = w.shape[-1]
    assert first - (rows - 1) >= 0 and first + cols <= M - 1
    lead = w.shape[:-1]
    flat = jnp.broadcast_to(w[..., None, :], lead + (rows, M)).reshape(lead + (rows * M,))
    return flat[..., :rows * (M - 1)].reshape(lead + (rows, M - 1))[..., first:first + cols]


def _t5_tables(t5_rel_bias):
    T, TK = DA_TQ, DA_TK
    base = T - 1 - DA_D_LO * TK
    first = LANES - 1
    assert (base - first) % LANES == 0
    M = base + DA_D_HI * TK + TK + 8
    rel = jnp.arange(M, dtype=jnp.int32) - base
    vec = jnp.transpose(t5_rel_bias[_t5_bucket(rel)].astype(jnp.float32) * LOG2E)
    band = _toeplitz(vec, LANES, first, M - 1 - first)
    tiles = [jnp.concatenate(
        [band[:, :, base - first + d * TK - a * LANES:][:, :, :TK] for a in range(T // LANES)], axis=1)
        for d in range(DA_D_LO, DA_D_HI + 1)]
    return jnp.stack(tiles, axis=1)


def _na_tables(na_rpb):
    n_dr, n_dc = 2 * NA_ROWS - 1, 2 * NA_COLS - 1
    c = jnp.arange(GRID_W, dtype=jnp.int32)[:, None]
    kc = jnp.arange(GRID_W, dtype=jnp.int32)[None, :]
    cs = jnp.clip(c - NA_COLS // 2, 0, GRID_W - NA_COLS)
    rpb = na_rpb.astype(jnp.float32)[:, :, :, None, None] * LOG2E
    band = jnp.zeros((NA_HEADS, n_dr, GRID_W, GRID_W), jnp.float32)
    for dc in range(n_dc):
        band = jnp.where(kc - c + (NA_COLS - 1) == dc, rpb[:, :, dc], band)
    band = jnp.where((kc >= cs) & (kc < cs + NA_COLS), band, NEG_BIG)
    per_var = [jnp.transpose(band[:, NA_ROWS - 1 - v:n_dr - v], (0, 2, 1, 3)) for v in range(NA_ROWS)]
    return jnp.stack(per_var, axis=0).reshape(NA_ROWS, NA_HEADS // NA_GROUP, NA_GROUP * GRID_W,
                                              NA_ROWS * GRID_W)


def _layer(x, tables, pre_w, post_w, w_in, lam, subln_w, w_o_diff, w_o_na, w_out, lam_init):
    B, S, D = x.shape
    t5_tiles, na_bias = tables
    x2d = x.reshape(B * S, D)
    proj2d = _proj(x2d, pre_w, w_in)
    proj = proj2d.reshape(B, S, IN_WIDTH)
    oa = _diff_attention(proj, lam, t5_tiles, subln_w, lam_init)
    on = _neighbourhood_attention(proj, na_bias)
    y = _output(x2d, oa.reshape(B * S, DA_WIDTH), on.reshape(B * S, NA_WIDTH), proj2d,
                w_o_diff, w_o_na, w_out, post_w)
    return y.reshape(B, S, D)


def _trunk(xs, t5_rel_bias, pre_norm_w, post_norm_w, w_in, lambda_q1, lambda_k1, lambda_q2,
           lambda_k2, subln_w, na_rpb, w_o_diff, w_o_na, w_out):
    depth = w_in.shape[0]
    t5_tiles = _t5_tables(t5_rel_bias)
    f32 = jnp.float32
    for l in range(depth):
        lam_init = 0.8 - 0.6 * math.exp(-0.3 * l)
        lam = (jnp.exp(jnp.sum(lambda_q1[l].astype(f32) * lambda_k1[l].astype(f32)))
               - jnp.exp(jnp.sum(lambda_q2[l].astype(f32) * lambda_k2[l].astype(f32)))
               + lam_init).reshape(1)
        tables = (t5_tiles, _na_tables(na_rpb[l]))
        args = (pre_norm_w[l].reshape(1, D_MODEL).astype(f32),
                post_norm_w[l].reshape(1, D_MODEL).astype(f32),
                w_in[l].astype(jnp.bfloat16), lam,
                subln_w[l].reshape(1, DA_V_DIM).astype(f32),
                w_o_diff[l].astype(jnp.bfloat16), w_o_na[l].astype(jnp.bfloat16),
                w_out[l].astype(jnp.bfloat16), lam_init)
        xs = [_layer(x, tables, *args) for x in xs]
    return xs


def kernel(x_prompt, x_sample, t5_rel_bias, pre_norm_w, post_norm_w, w_in, lambda_q1, lambda_k1,
           lambda_q2, lambda_k2, subln_w, na_rpb, w_o_diff, w_o_na, w_out):
    y_prompt, y_sample = _trunk([x_prompt, x_sample], t5_rel_bias, pre_norm_w, post_norm_w, w_in,
                                lambda_q1, lambda_k1, lambda_q2, lambda_k2, subln_w, na_rpb,
                                w_o_diff, w_o_na, w_out)
    return (y_prompt, y_sample)
```

```python
import functools
import math

import jax
import jax.numpy as jnp
from jax import lax
from jax.experimental import pallas as pl
from jax.experimental.pallas import tpu as pltpu

D_MODEL = 1024
DA_HEADS = 4
DA_HEAD_DIM = 64
DA_V_DIM = 2 * DA_HEAD_DIM
DA_WIDTH = DA_HEADS * DA_V_DIM
T5_BUCKETS = 32
T5_MAX_DIST = 128
GRID_W = 64
NA_HEADS = 8
NA_HEAD_DIM = 64
NA_WIDTH = NA_HEADS * NA_HEAD_DIM
NA_ROWS = 8
NA_COLS = 16
IN_WIDTH = 4 * DA_WIDTH + 4 * NA_WIDTH + 2 * D_MODEL
NORM_EPS = 1e-6
SUBLN_EPS = 1e-5
LOG2E = math.log2(math.e)

LANES = 128
VMEM_LIMIT_BYTES = 56 * 1024 * 1024

PROJ_TM = 512
PROJ_CHUNK = 512
DA_TQ = 512
DA_TK = 256
DA_D_LO = -((T5_MAX_DIST + DA_TK - 1 + DA_TK - 1) // DA_TK)
DA_D_HI = (T5_MAX_DIST + DA_TQ - 1 + DA_TK - 1) // DA_TK
DA_BIAS_TILES = DA_D_HI - DA_D_LO + 1
NA_RB = 8
NA_GROUP = 4
OUT_TM = 512
NEG_BIG = -1e30

_QA, _KA, _VA = 0, DA_HEADS, 2 * DA_HEADS
_ZA_BLK, _QN_BLK, _KN_BLK, _VN_BLK, _ZN_BLK = 3, 4, 5, 6, 7
_G_BLK = 2

_NT = (((1,), (1,)), ((), ()))


def _params(sem):
    return pltpu.CompilerParams(dimension_semantics=sem, vmem_limit_bytes=VMEM_LIMIT_BYTES)


def _sigmoid(x):
    return 0.5 * jnp.tanh(0.5 * x) + 0.5


def _proj_kernel(x_ref, pw_ref, w_ref, o_ref, *, finish):
    x = x_ref[...]
    ms = jnp.mean(x * x, axis=-1, keepdims=True)
    h = (x * lax.rsqrt(ms + NORM_EPS) * pw_ref[...]).astype(jnp.bfloat16)
    for c, how in enumerate(finish):
        cols = slice(c * PROJ_CHUNK, (c + 1) * PROJ_CHUNK)
        acc = jnp.dot(h, w_ref[:, cols], preferred_element_type=jnp.float32)
        if how == "q":
            acc = acc * (LOG2E * DA_HEAD_DIM ** -0.5)
        elif how == "silu":
            acc = acc * _sigmoid(acc)
        elif how == "sigmoid":
            acc = _sigmoid(acc)
        o_ref[:, cols] = acc.astype(jnp.bfloat16)


def _proj(x2d, pre_w, w_in_bf16):
    tokens = x2d.shape[0]
    assert tokens % PROJ_TM == 0
    assert DA_HEAD_DIM == NA_HEAD_DIM and DA_WIDTH == PROJ_CHUNK and NA_WIDTH == PROJ_CHUNK
    per_branch = ["q", "", "", "silu"]
    finish = per_branch * 2 + ["sigmoid"] * (2 * D_MODEL // PROJ_CHUNK)
    assert len(finish) == IN_WIDTH // PROJ_CHUNK
    return pl.pallas_call(
        functools.partial(_proj_kernel, finish=tuple(finish)),
        grid=(tokens // PROJ_TM,),
        in_specs=[
            pl.BlockSpec((PROJ_TM, D_MODEL), lambda i: (i, 0)),
            pl.BlockSpec((1, D_MODEL), lambda i: (0, 0)),
            pl.BlockSpec((D_MODEL, IN_WIDTH), lambda i: (0, 0)),
        ],
        out_specs=pl.BlockSpec((PROJ_TM, IN_WIDTH), lambda i: (i, 0)),
        out_shape=jax.ShapeDtypeStruct((tokens, IN_WIDTH), jnp.bfloat16),
        compiler_params=_params(("parallel",)),
        name="proj",
    )(x2d, pre_w, w_in_bf16)


def _lane_groups(x):
    return [x[:, g * LANES:(g + 1) * LANES] for g in range(x.shape[1] // LANES)]


def _da_kernel(lam_ref, q_ref, k_ref, v_ref, bias_ref, sw_ref, o_ref,
               s_scr, q2_scr, mx_scr, m_scr, l_scr, acc_scr, *, n_chunks, n_blocks, out_scale):
    T = DA_TQ
    TK = DA_TK
    t = pl.program_id(0)

    @pl.when(t == 0)
    def _():
        s_scr[...] = jnp.zeros(s_scr.shape, jnp.float32)
        m_scr[...] = jnp.zeros(m_scr.shape, jnp.float32)
        l_scr[...] = jnp.ones(l_scr.shape, jnp.float32)
        acc_scr[...] = jnp.zeros(acc_scr.shape, jnp.float32)

    c0 = lax.rem(jnp.minimum(t, n_blocks - 1), n_chunks * TK // T) * (T // TK)
    lane = lax.broadcasted_iota(jnp.int32, (T, LANES), 1)

    def lane_tile(x):
        return jnp.concatenate([x] * (TK // LANES), axis=1)

    def group_reduce(x, op):
        groups = _lane_groups(x)
        out = groups[0]
        for g in groups[1:]:
            out = op(out, g)
        return out

    def step(cur):
        prv = 1 - cur
        q = q_ref[0]
        zero = jnp.zeros_like(q)
        q2_scr[:T] = jnp.where(lane < DA_HEAD_DIM, q, zero)
        q2_scr[T:] = jnp.where(lane >= DA_HEAD_DIM, q, zero)
        mx_scr[...] = jnp.full(mx_scr.shape, -jnp.inf, jnp.float32)

        l0 = jnp.sum(l_scr[cur, 0], axis=-1, keepdims=True)
        l1 = jnp.sum(l_scr[cur, 1], axis=-1, keepdims=True)
        o = acc_scr[:T] / l0 - lam_ref[0] * (acc_scr[T:] / l1)
        ms = jnp.mean(o * o, axis=-1, keepdims=True)
        o = o * lax.rsqrt(ms + SUBLN_EPS) * sw_ref[...] * out_scale
        o_ref[0] = o.astype(jnp.bfloat16)
        l_scr[prv] = jnp.zeros((2, T, LANES), jnp.float32)

        def stage1(c):
            bias = bias_ref[0, jnp.clip(c - c0, DA_D_LO, DA_D_HI) - DA_D_LO]
            s2 = lax.dot_general(q2_scr[...], k_ref[0, c * TK:(c + 1) * TK, :], _NT,
                                 preferred_element_type=jnp.float32)
            for m in range(2):
                s = s2[m * T:(m + 1) * T] + bias
                s_scr[cur, m, c] = s
                mx_scr[m] = jnp.maximum(mx_scr[m], group_reduce(s, jnp.maximum))

        def stage2(c, acc):
            ps = []
            for m in range(2):
                p = jnp.exp2(s_scr[prv, m, c] - lane_tile(m_scr[prv, m]))
                l_scr[prv, m] = l_scr[prv, m] + group_reduce(p, jnp.add)
                ps.append(p.astype(jnp.bfloat16))
            return acc + jnp.dot(jnp.concatenate(ps, axis=0), v_ref[0, c * TK:(c + 1) * TK, :],
                                 preferred_element_type=jnp.float32)

        acc = jnp.zeros((2 * T, DA_V_DIM), jnp.float32)
        for c in range(n_chunks):
            stage1(c)
            acc = stage2(c, acc)
        for m in range(2):
            row_max = jnp.max(mx_scr[m], axis=-1, keepdims=True)
            m_scr[cur, m] = jnp.broadcast_to(row_max, (T, LANES))
        acc_scr[...] = acc

    parity = lax.rem(t, 2)
    for cur in range(2):
        pl.when(parity == cur)(functools.partial(step, cur))


def _diff_attention(proj, lam, bias_tiles, subln_w, lam_init):
    B, S, _ = proj.shape
    T, TK = DA_TQ, DA_TK
    assert S % T == 0 and T % TK == 0
    n = S // TK
    nq = S // T
    n_blocks = B * DA_HEADS * nq

    def block(t, lag):
        u = jnp.clip(t - lag, 0, n_blocks - 1)
        return u // (DA_HEADS * nq), (u // nq) % DA_HEADS, u % nq

    def q_map(t):
        b, h, i = block(t, 0)
        return b, i, _QA + h

    def k_map(t):
        b, h, _ = block(t, 0)
        return b, 0, _KA + h

    def v_map(t):
        b, h, _ = block(t, 1)
        return b, 0, _VA + h

    def bias_map(t):
        _, h, _ = block(t, 0)
        return h, 0, 0, 0

    def o_map(t):
        b, h, i = block(t, 2)
        return b, i, h

    return pl.pallas_call(
        functools.partial(_da_kernel, n_chunks=n, n_blocks=n_blocks, out_scale=1.0 - lam_init),
        grid=(n_blocks + 2,),
        in_specs=[
            pl.BlockSpec(memory_space=pltpu.SMEM),
            pl.BlockSpec((1, T, LANES), q_map),
            pl.BlockSpec((1, S, LANES), k_map),
            pl.BlockSpec((1, S, LANES), v_map),
            pl.BlockSpec((1, DA_BIAS_TILES, T, TK), bias_map),
            pl.BlockSpec((1, DA_V_DIM), lambda t: (0, 0)),
        ],
        out_specs=pl.BlockSpec((1, T, LANES), o_map),
        out_shape=jax.ShapeDtypeStruct((B, S, DA_WIDTH), jnp.bfloat16),
        scratch_shapes=[
            pltpu.VMEM((2, 2, n, T, TK), jnp.float32),
            pltpu.VMEM((2 * T, LANES), jnp.bfloat16),
            pltpu.VMEM((2, T, LANES), jnp.float32),
            pltpu.VMEM((2, 2, T, LANES), jnp.float32),
            pltpu.VMEM((2, 2, T, LANES), jnp.float32),
            pltpu.VMEM((2 * T, DA_V_DIM), jnp.float32),
        ],
        compiler_params=_params(("arbitrary",)),
        name="diff_attn",
    )(lam, proj, proj, proj, bias_tiles, subln_w)


def _na_kernel(q_ref, k_ref, v_ref, bias_ref, o_ref, *, rows):
    rb = pl.program_id(1)
    win = NA_ROWS * GRID_W
    gw = NA_GROUP * NA_HEAD_DIM
    head_of_lane = lax.broadcasted_iota(jnp.int32, (GRID_W, gw), 1) // NA_HEAD_DIM

    def row_body(i, carry):
        r = rb * NA_RB + i
        start = jnp.clip(r - NA_ROWS // 2, 0, rows - NA_ROWS)
        var = r - start
        krows = pl.ds(pl.multiple_of(start * GRID_W, GRID_W), win)
        qrows = pl.ds(pl.multiple_of(i * GRID_W, GRID_W), GRID_W)
        for g in range(NA_HEADS // NA_GROUP):
            cols = slice(g * gw, (g + 1) * gw)
            qg = q_ref[0, qrows, cols]
            zero = jnp.zeros_like(qg)
            qbd = jnp.concatenate([jnp.where(head_of_lane == h, qg, zero) for h in range(NA_GROUP)],
                                  axis=0)
            s = lax.dot_general(qbd, k_ref[0, krows, cols], _NT, preferred_element_type=jnp.float32)
            s = s + bias_ref[var, g]
            m = jnp.max(s, axis=-1, keepdims=True)
            p = jnp.exp2(s - m)
            l = jnp.sum(p, axis=-1, keepdims=True)
            pv = jnp.dot(p.astype(jnp.bfloat16), v_ref[0, krows, cols],
                         preferred_element_type=jnp.float32) / l
            o = pv[:GRID_W]
            for h in range(1, NA_GROUP):
                o = jnp.where(head_of_lane == h, pv[h * GRID_W:(h + 1) * GRID_W], o)
            o_ref[0, qrows, cols] = o.astype(jnp.bfloat16)
        return carry

    lax.fori_loop(0, NA_RB, row_body, 0, unroll=True)


def _neighbourhood_attention(proj, na_bias):
    B, S, _ = proj.shape
    rows = S // GRID_W
    assert S % GRID_W == 0 and rows >= NA_ROWS and rows % NA_RB == 0
    blk = NA_RB * GRID_W
    return pl.pallas_call(
        functools.partial(_na_kernel, rows=rows),
        grid=(B, rows // NA_RB),
        in_specs=[
            pl.BlockSpec((1, blk, NA_WIDTH), lambda b, i: (b, i, _QN_BLK)),
            pl.BlockSpec((1, S, NA_WIDTH), lambda b, i: (b, 0, _KN_BLK)),
            pl.BlockSpec((1, S, NA_WIDTH), lambda b, i: (b, 0, _VN_BLK)),
            pl.BlockSpec(na_bias.shape, lambda b, i: (0, 0, 0, 0)),
        ],
        out_specs=pl.BlockSpec((1, blk, NA_WIDTH), lambda b, i: (b, i, 0)),
        out_shape=jax.ShapeDtypeStruct((B, S, NA_WIDTH), jnp.bfloat16),
        compiler_params=_params(("parallel", "parallel")),
        name="nbr_attn",
    )(proj, proj, proj, na_bias)


def _out_kernel(x_ref, oa_ref, on_ref, za_ref, zn_ref, g_ref, woa_ref, won_ref, wout_ref,
                pw_ref, y_ref):
    def branch(o_ref, z_ref, w_ref):
        return jnp.dot(o_ref[...] * z_ref[...], w_ref[...], preferred_element_type=jnp.float32)

    ya = branch(oa_ref, za_ref, woa_ref)
    yn = branch(on_ref, zn_ref, won_ref)
    ga = g_ref[:, :D_MODEL].astype(jnp.float32)
    gn = g_ref[:, D_MODEL:].astype(jnp.float32)
    merged = (ga * ya + gn * yn).astype(jnp.bfloat16)
    out = jnp.dot(merged, wout_ref[...], preferred_element_type=jnp.float32)
    ms = jnp.mean(out * out, axis=-1, keepdims=True)
    y_ref[...] = x_ref[...] + out * lax.rsqrt(ms + NORM_EPS) * pw_ref[...]


def _output(x2d, oa2d, on2d, proj2d, w_o_diff, w_o_na, w_out, post_w):
    tokens = x2d.shape[0]
    tm = OUT_TM
    assert tokens % tm == 0

    def const(shape):
        return pl.BlockSpec(shape, lambda i: (0, 0))

    return pl.pallas_call(
        _out_kernel,
        grid=(tokens // tm,),
        in_specs=[
            pl.BlockSpec((tm, D_MODEL), lambda i: (i, 0)),
            pl.BlockSpec((tm, DA_WIDTH), lambda i: (i, 0)),
            pl.BlockSpec((tm, NA_WIDTH), lambda i: (i, 0)),
            pl.BlockSpec((tm, DA_WIDTH), lambda i: (i, _ZA_BLK)),
            pl.BlockSpec((tm, NA_WIDTH), lambda i: (i, _ZN_BLK)),
            pl.BlockSpec((tm, 2 * D_MODEL), lambda i: (i, _G_BLK)),
            const((DA_WIDTH, D_MODEL)),
            const((NA_WIDTH, D_MODEL)),
            const((D_MODEL, D_MODEL)),
            const((1, D_MODEL)),
        ],
        out_specs=pl.BlockSpec((tm, D_MODEL), lambda i: (i, 0)),
        out_shape=jax.ShapeDtypeStruct((tokens, D_MODEL), jnp.float32),
        compiler_params=_params(("parallel",)),
        name="out_proj",
    )(x2d, oa2d, on2d, proj2d, proj2d, proj2d, w_o_diff, w_o_na, w_out, post_w)


def _t5_bucket(rel):
    nb = T5_BUCKETS // 2
    ret = jnp.where(rel > 0, nb, 0)
    n = jnp.abs(rel)
    max_exact = nb // 2
    nf = jnp.maximum(n, 1).astype(jnp.float32)
    large = max_exact + (jnp.log(nf / max_exact) / math.log(T5_MAX_DIST / max_exact)
                         * (nb - max_exact)).astype(jnp.int32)
    large = jnp.minimum(large, nb - 1)
    return ret + jnp.where(n < max_exact, n, large)


def _toeplitz(w, rows, first, cols):
    M = w.shape[-1]
    assert first - (rows - 1) >= 0 and first + cols <= M - 1
    lead = w.shape[:-1]
    flat = jnp.broadcast_to(w[..., None, :], lead + (rows, M)).reshape(lead + (rows * M,))
    return flat[..., :rows * (M - 1)].reshape(lead + (rows, M - 1))[..., first:first + cols]


def _t5_tables(t5_rel_bias):
    T, TK = DA_TQ, DA_TK
    base = T - 1 - DA_D_LO * TK
    first = LANES - 1
    assert (base - first) % LANES == 0
    M = base + DA_D_HI * TK + TK + 8
    rel = jnp.arange(M, dtype=jnp.int32) - base
    vec = jnp.transpose(t5_rel_bias[_t5_bucket(rel)].astype(jnp.float32) * LOG2E)
    band = _toeplitz(vec, LANES, first, M - 1 - first)
    tiles = [jnp.concatenate(
        [band[:, :, base - first + d * TK - a * LANES:][:, :, :TK] for a in range(T // LANES)], axis=1)
        for d in range(DA_D_LO, DA_D_HI + 1)]
    return jnp.stack(tiles, axis=1)


def _na_tables(na_rpb):
    n_dr, n_dc = 2 * NA_ROWS - 1, 2 * NA_COLS - 1
    c = jnp.arange(GRID_W, dtype=jnp.int32)[:, None]
    kc = jnp.arange(GRID_W, dtype=jnp.int32)[None, :]
    cs = jnp.clip(c - NA_COLS // 2, 0, GRID_W - NA_COLS)
    rpb = na_rpb.astype(jnp.float32)[:, :, :, None, None] * LOG2E
    band = jnp.zeros((NA_HEADS, n_dr, GRID_W, GRID_W), jnp.float32)
    for dc in range(n_dc):
        band = jnp.where(kc - c + (NA_COLS - 1) == dc, rpb[:, :, dc], band)
    band = jnp.where((kc >= cs) & (kc < cs + NA_COLS), band, NEG_BIG)
    per_var = [jnp.transpose(band[:, NA_ROWS - 1 - v:n_dr - v], (0, 2, 1, 3)) for v in range(NA_ROWS)]
    return jnp.stack(per_var, axis=0).reshape(NA_ROWS, NA_HEADS // NA_GROUP, NA_GROUP * GRID_W,
                                              NA_ROWS * GRID_W)


def _layer(x, tables, pre_w, post_w, w_in, lam, subln_w, w_o_diff, w_o_na, w_out, lam_init):
    B, S, D = x.shape
    t5_tiles, na_bias = tables
    x2d = x.reshape(B * S, D)
    proj2d = _proj(x2d, pre_w, w_in)
    proj = proj2d.reshape(B, S, IN_WIDTH)
    oa = _diff_attention(proj, lam, t5_tiles, subln_w, lam_init)
    on = _neighbourhood_attention(proj, na_bias)
    y = _output(x2d, oa.reshape(B * S, DA_WIDTH), on.reshape(B * S, NA_WIDTH), proj2d,
                w_o_diff, w_o_na, w_out, post_w)
    return y.reshape(B, S, D)


def _trunk(xs, t5_rel_bias, pre_norm_w, post_norm_w, w_in, lambda_q1, lambda_k1, lambda_q2,
           lambda_k2, subln_w, na_rpb, w_o_diff, w_o_na, w_out):
    depth = w_in.shape[0]
    t5_tiles = _t5_tables(t5_rel_bias)
    f32 = jnp.float32
    for l in range(depth):
        lam_init = 0.8 - 0.6 * math.exp(-0.3 * l)
        lam = (jnp.exp(jnp.sum(lambda_q1[l].astype(f32) * lambda_k1[l].astype(f32)))
               - jnp.exp(jnp.sum(lambda_q2[l].astype(f32) * lambda_k2[l].astype(f32)))
               + lam_init).reshape(1)
        tables = (t5_tiles, _na_tables(na_rpb[l]))
        args = (pre_norm_w[l].reshape(1, D_MODEL).astype(f32),
                post_norm_w[l].reshape(1, D_MODEL).astype(f32),
                w_in[l].astype(jnp.bfloat16), lam,
                subln_w[l].reshape(1, DA_V_DIM).astype(f32),
                w_o_diff[l].astype(jnp.bfloat16), w_o_na[l].astype(jnp.bfloat16),
                w_out[l].astype(jnp.bfloat16), lam_init)
        xs = [_layer(x, tables, *args) for x in xs]
    return xs


def kernel(x_prompt, x_sample, t5_rel_bias, pre_norm_w, post_norm_w, w_in, lambda_q1, lambda_k1,
           lambda_q2, lambda_k2, subln_w, na_rpb, w_o_diff, w_o_na, w_out):
    y_prompt, y_sample = _trunk([x_prompt, x_sample], t5_rel_bias, pre_norm_w, post_norm_w, w_in,
                                lambda_q1, lambda_k1, lambda_q2, lambda_k2, subln_w, na_rpb,
                                w_o_diff, w_o_na, w_out)
    return (y_prompt, y_sample)
```

```python
import functools
import math

import jax
import jax.numpy as jnp
from jax import lax
from jax.experimental import pallas as pl
from jax.experimental.pallas import tpu as pltpu

D_MODEL = 1024
DA_HEADS = 4
DA_HEAD_DIM = 64
DA_V_DIM = 2 * DA_HEAD_DIM
DA_WIDTH = DA_HEADS * DA_V_DIM
T5_BUCKETS = 32
T5_MAX_DIST = 128
GRID_W = 64
NA_HEADS = 8
NA_HEAD_DIM = 64
NA_WIDTH = NA_HEADS * NA_HEAD_DIM
NA_ROWS = 8
NA_COLS = 16
IN_WIDTH = 4 * DA_WIDTH + 4 * NA_WIDTH + 2 * D_MODEL
NORM_EPS = 1e-6
SUBLN_EPS = 1e-5
LOG2E = math.log2(math.e)

LANES = 128
VMEM_LIMIT_BYTES = 56 * 1024 * 1024

PROJ_TM = 512
PROJ_CHUNK = 512
DA_TQ = 512
DA_TK = 256
DA_D_LO = -((T5_MAX_DIST + DA_TK - 1 + DA_TK - 1) // DA_TK)
DA_D_HI = (T5_MAX_DIST + DA_TQ - 1 + DA_TK - 1) // DA_TK
DA_BIAS_TILES = DA_D_HI - DA_D_LO + 1
NA_RB = 16
NA_GROUP = 4
OUT_TM = 512
NEG_BIG = -1e30

_QA, _KA, _VA = 0, DA_HEADS, 2 * DA_HEADS
_ZA_BLK, _QN_BLK, _KN_BLK, _VN_BLK, _ZN_BLK = 3, 4, 5, 6, 7
_G_BLK = 2

_NT = (((1,), (1,)), ((), ()))


def _params(sem):
    return pltpu.CompilerParams(dimension_semantics=sem, vmem_limit_bytes=VMEM_LIMIT_BYTES)


def _proj_kernel(x_ref, pw_ref, w_ref, o_ref, *, scales):
    x = x_ref[...]
    ms = jnp.mean(x * x, axis=-1, keepdims=True)
    h = (x * lax.rsqrt(ms + NORM_EPS) * pw_ref[...]).astype(jnp.bfloat16)
    for c, sc in enumerate(scales):
        cols = slice(c * PROJ_CHUNK, (c + 1) * PROJ_CHUNK)
        acc = jnp.dot(h, w_ref[:, cols], preferred_element_type=jnp.float32)
        if sc != 1.0:
            acc = acc * sc
        o_ref[:, cols] = acc.astype(jnp.bfloat16)


def _proj(x2d, pre_w, w_in_bf16):
    tokens = x2d.shape[0]
    assert tokens % PROJ_TM == 0
    qscale = LOG2E * DA_HEAD_DIM ** -0.5
    assert DA_HEAD_DIM == NA_HEAD_DIM
    scales = [1.0] * (IN_WIDTH // PROJ_CHUNK)
    scales[0] = qscale
    scales[(4 * DA_WIDTH) // PROJ_CHUNK] = qscale
    assert DA_WIDTH == PROJ_CHUNK and NA_WIDTH == PROJ_CHUNK
    return pl.pallas_call(
        functools.partial(_proj_kernel, scales=tuple(scales)),
        grid=(tokens // PROJ_TM,),
        in_specs=[
            pl.BlockSpec((PROJ_TM, D_MODEL), lambda i: (i, 0)),
            pl.BlockSpec((1, D_MODEL), lambda i: (0, 0)),
            pl.BlockSpec((D_MODEL, IN_WIDTH), lambda i: (0, 0)),
        ],
        out_specs=pl.BlockSpec((PROJ_TM, IN_WIDTH), lambda i: (i, 0)),
        out_shape=jax.ShapeDtypeStruct((tokens, IN_WIDTH), jnp.bfloat16),
        compiler_params=_params(("parallel",)),
        name="proj",
    )(x2d, pre_w, w_in_bf16)


def _lane_groups(x):
    return [x[:, g * LANES:(g + 1) * LANES] for g in range(x.shape[1] // LANES)]


def _da_kernel(lam_ref, q_ref, k_ref, v_ref, bias_ref, sw_ref, o_ref,
               s_scr, q2_scr, mx_scr, m_scr, l_scr, acc_scr, *, n_chunks, n_blocks, out_scale):
    T = DA_TQ
    TK = DA_TK
    t = pl.program_id(0)

    @pl.when(t == 0)
    def _():
        s_scr[...] = jnp.zeros(s_scr.shape, jnp.float32)
        m_scr[...] = jnp.zeros(m_scr.shape, jnp.float32)
        l_scr[...] = jnp.ones(l_scr.shape, jnp.float32)
        acc_scr[...] = jnp.zeros(acc_scr.shape, jnp.float32)

    c0 = lax.rem(jnp.minimum(t, n_blocks - 1), n_chunks * TK // T) * (T // TK)
    lane = lax.broadcasted_iota(jnp.int32, (T, LANES), 1)

    def lane_tile(x):
        return jnp.concatenate([x] * (TK // LANES), axis=1)

    def group_reduce(x, op):
        groups = _lane_groups(x)
        out = groups[0]
        for g in groups[1:]:
            out = op(out, g)
        return out

    def step(cur):
        prv = 1 - cur
        q = q_ref[0]
        zero = jnp.zeros_like(q)
        q2_scr[:T] = jnp.where(lane < DA_HEAD_DIM, q, zero)
        q2_scr[T:] = jnp.where(lane >= DA_HEAD_DIM, q, zero)
        mx_scr[...] = jnp.full(mx_scr.shape, -jnp.inf, jnp.float32)

        l0 = jnp.sum(l_scr[cur, 0], axis=-1, keepdims=True)
        l1 = jnp.sum(l_scr[cur, 1], axis=-1, keepdims=True)
        o = acc_scr[:T] / l0 - lam_ref[0] * (acc_scr[T:] / l1)
        ms = jnp.mean(o * o, axis=-1, keepdims=True)
        o = o * lax.rsqrt(ms + SUBLN_EPS) * sw_ref[...] * out_scale
        o_ref[0] = o.astype(jnp.bfloat16)
        l_scr[prv] = jnp.zeros((2, T, LANES), jnp.float32)

        def stage1(c):
            bias = bias_ref[0, jnp.clip(c - c0, DA_D_LO, DA_D_HI) - DA_D_LO]
            s2 = lax.dot_general(q2_scr[...], k_ref[0, c * TK:(c + 1) * TK, :], _NT,
                                 preferred_element_type=jnp.float32)
            for m in range(2):
                s = s2[m * T:(m + 1) * T] + bias
                s_scr[cur, m, c] = s
                mx_scr[m] = jnp.maximum(mx_scr[m], group_reduce(s, jnp.maximum))

        def stage2(c, acc):
            ps = []
            for m in range(2):
                p = jnp.exp2(s_scr[prv, m, c] - lane_tile(m_scr[prv, m]))
                l_scr[prv, m] = l_scr[prv, m] + group_reduce(p, jnp.add)
                ps.append(p.astype(jnp.bfloat16))
            return acc + jnp.dot(jnp.concatenate(ps, axis=0), v_ref[0, c * TK:(c + 1) * TK, :],
                                 preferred_element_type=jnp.float32)

        acc = jnp.zeros((2 * T, DA_V_DIM), jnp.float32)
        for c in range(n_chunks):
            stage1(c)
            acc = stage2(c, acc)
        for m in range(2):
            row_max = jnp.max(mx_scr[m], axis=-1, keepdims=True)
            m_scr[cur, m] = jnp.broadcast_to(row_max, (T, LANES))
        acc_scr[...] = acc

    parity = lax.rem(t, 2)
    for cur in range(2):
        pl.when(parity == cur)(functools.partial(step, cur))


def _diff_attention(proj, lam, bias_tiles, subln_w, lam_init):
    B, S, _ = proj.shape
    T, TK = DA_TQ, DA_TK
    assert S % T == 0 and T % TK == 0
    n = S // TK
    nq = S // T
    n_blocks = B * DA_HEADS * nq

    def block(t, lag):
        u = jnp.clip(t - lag, 0, n_blocks - 1)
        return lax.div(u, DA_HEADS * nq), lax.rem(lax.div(u, nq), DA_HEADS), lax.rem(u, nq)

    def q_map(t):
        b, h, i = block(t, 0)
        return b, i, _QA + h

    def k_map(t):
        b, h, _ = block(t, 0)
        return b, 0, _KA + h

    def v_map(t):
        b, h, _ = block(t, 1)
        return b, 0, _VA + h

    def bias_map(t):
        _, h, _ = block(t, 0)
        return h, 0, 0, 0

    def o_map(t):
        b, h, i = block(t, 2)
        return b, i, h

    return pl.pallas_call(
        functools.partial(_da_kernel, n_chunks=n, n_blocks=n_blocks, out_scale=1.0 - lam_init),
        grid=(n_blocks + 2,),
        in_specs=[
            pl.BlockSpec(memory_space=pltpu.SMEM),
            pl.BlockSpec((1, T, LANES), q_map),
            pl.BlockSpec((1, S, LANES), k_map),
            pl.BlockSpec((1, S, LANES), v_map),
            pl.BlockSpec((1, DA_BIAS_TILES, T, TK), bias_map),
            pl.BlockSpec((1, DA_V_DIM), lambda t: (0, 0)),
        ],
        out_specs=pl.BlockSpec((1, T, LANES), o_map),
        out_shape=jax.ShapeDtypeStruct((B, S, DA_WIDTH), jnp.bfloat16),
        scratch_shapes=[
            pltpu.VMEM((2, 2, n, T, TK), jnp.float32),
            pltpu.VMEM((2 * T, LANES), jnp.bfloat16),
            pltpu.VMEM((2, T, LANES), jnp.float32),
            pltpu.VMEM((2, 2, T, LANES), jnp.float32),
            pltpu.VMEM((2, 2, T, LANES), jnp.float32),
            pltpu.VMEM((2 * T, DA_V_DIM), jnp.float32),
        ],
        compiler_params=_params(("arbitrary",)),
        name="diff_attn",
    )(lam, proj, proj, proj, bias_tiles, subln_w)


def _na_kernel(q_ref, k_ref, v_ref, bias_ref, o_ref, *, rows):
    rb = pl.program_id(1)
    win = NA_ROWS * GRID_W
    gw = NA_GROUP * NA_HEAD_DIM
    head_of_lane = lax.broadcasted_iota(jnp.int32, (GRID_W, gw), 1) // NA_HEAD_DIM

    def row_body(i, carry):
        r = rb * NA_RB + i
        start = jnp.clip(r - NA_ROWS // 2, 0, rows - NA_ROWS)
        var = r - start
        krows = pl.ds(pl.multiple_of(start * GRID_W, GRID_W), win)
        qrows = pl.ds(pl.multiple_of(i * GRID_W, GRID_W), GRID_W)
        for g in range(NA_HEADS // NA_GROUP):
            cols = slice(g * gw, (g + 1) * gw)
            qg = q_ref[0, qrows, cols]
            zero = jnp.zeros_like(qg)
            qbd = jnp.concatenate([jnp.where(head_of_lane == h, qg, zero) for h in range(NA_GROUP)],
                                  axis=0)
            s = lax.dot_general(qbd, k_ref[0, krows, cols], _NT, preferred_element_type=jnp.float32)
            s = s + bias_ref[var, g]
            m = jnp.max(s, axis=-1, keepdims=True)
            p = jnp.exp2(s - m)
            l = jnp.sum(p, axis=-1, keepdims=True)
            pv = jnp.dot(p.astype(jnp.bfloat16), v_ref[0, krows, cols],
                         preferred_element_type=jnp.float32) / l
            o = pv[:GRID_W]
            for h in range(1, NA_GROUP):
                o = jnp.where(head_of_lane == h, pv[h * GRID_W:(h + 1) * GRID_W], o)
            o_ref[0, qrows, cols] = o.astype(jnp.bfloat16)
        return carry

    lax.fori_loop(0, NA_RB, row_body, 0, unroll=True)


def _neighbourhood_attention(proj, na_bias):
    B, S, _ = proj.shape
    rows = S // GRID_W
    assert S % GRID_W == 0 and rows >= NA_ROWS and rows % NA_RB == 0
    blk = NA_RB * GRID_W
    return pl.pallas_call(
        functools.partial(_na_kernel, rows=rows),
        grid=(B, rows // NA_RB),
        in_specs=[
            pl.BlockSpec((1, blk, NA_WIDTH), lambda b, i: (b, i, _QN_BLK)),
            pl.BlockSpec((1, S, NA_WIDTH), lambda b, i: (b, 0, _KN_BLK)),
            pl.BlockSpec((1, S, NA_WIDTH), lambda b, i: (b, 0, _VN_BLK)),
            pl.BlockSpec(na_bias.shape, lambda b, i: (0, 0, 0, 0)),
        ],
        out_specs=pl.BlockSpec((1, blk, NA_WIDTH), lambda b, i: (b, i, 0)),
        out_shape=jax.ShapeDtypeStruct((B, S, NA_WIDTH), jnp.bfloat16),
        compiler_params=_params(("parallel", "parallel")),
        name="nbr_attn",
    )(proj, proj, proj, na_bias)


def _out_kernel(x_ref, oa_ref, on_ref, za_ref, zn_ref, g_ref, woa_ref, won_ref, wout_ref,
                pw_ref, y_ref):
    def branch(o_ref, z_ref, w_ref):
        z = z_ref[...].astype(jnp.float32)
        u = o_ref[...].astype(jnp.float32) * (z * jax.nn.sigmoid(z))
        return jnp.dot(u.astype(jnp.bfloat16), w_ref[...], preferred_element_type=jnp.float32)

    ya = branch(oa_ref, za_ref, woa_ref)
    yn = branch(on_ref, zn_ref, won_ref)
    ga = jax.nn.sigmoid(g_ref[:, :D_MODEL].astype(jnp.float32))
    gn = jax.nn.sigmoid(g_ref[:, D_MODEL:].astype(jnp.float32))
    merged = (ga * ya + gn * yn).astype(jnp.bfloat16)
    out = jnp.dot(merged, wout_ref[...], preferred_element_type=jnp.float32)
    ms = jnp.mean(out * out, axis=-1, keepdims=True)
    y_ref[...] = x_ref[...] + out * lax.rsqrt(ms + NORM_EPS) * pw_ref[...]


def _output(x2d, oa2d, on2d, proj2d, w_o_diff, w_o_na, w_out, post_w):
    tokens = x2d.shape[0]
    tm = OUT_TM
    assert tokens % tm == 0

    def const(shape):
        return pl.BlockSpec(shape, lambda i: (0, 0))

    return pl.pallas_call(
        _out_kernel,
        grid=(tokens // tm,),
        in_specs=[
            pl.BlockSpec((tm, D_MODEL), lambda i: (i, 0)),
            pl.BlockSpec((tm, DA_WIDTH), lambda i: (i, 0)),
            pl.BlockSpec((tm, NA_WIDTH), lambda i: (i, 0)),
            pl.BlockSpec((tm, DA_WIDTH), lambda i: (i, _ZA_BLK)),
            pl.BlockSpec((tm, NA_WIDTH), lambda i: (i, _ZN_BLK)),
            pl.BlockSpec((tm, 2 * D_MODEL), lambda i: (i, _G_BLK)),
            const((DA_WIDTH, D_MODEL)),
            const((NA_WIDTH, D_MODEL)),
            const((D_MODEL, D_MODEL)),
            const((1, D_MODEL)),
        ],
        out_specs=pl.BlockSpec((tm, D_MODEL), lambda i: (i, 0)),
        out_shape=jax.ShapeDtypeStruct((tokens, D_MODEL), jnp.float32),
        compiler_params=_params(("parallel",)),
        name="out_proj",
    )(x2d, oa2d, on2d, proj2d, proj2d, proj2d, w_o_diff, w_o_na, w_out, post_w)


def _t5_bucket(rel):
    nb = T5_BUCKETS // 2
    ret = jnp.where(rel > 0, nb, 0)
    n = jnp.abs(rel)
    max_exact = nb // 2
    nf = jnp.maximum(n, 1).astype(jnp.float32)
    large = max_exact + (jnp.log(nf / max_exact) / math.log(T5_MAX_DIST / max_exact)
                         * (nb - max_exact)).astype(jnp.int32)
    large = jnp.minimum(large, nb - 1)
    return ret + jnp.where(n < max_exact, n, large)


def _toeplitz(w, rows, first, cols):
    M = w.shape[-1]
    assert first - (rows - 1) >= 0 and first + cols <= M - 1
    lead = w.shape[:-1]
    flat = jnp.broadcast_to(w[..., None, :], lead + (rows, M)).reshape(lead + (rows * M,))
    return flat[..., :rows * (M - 1)].reshape(lead + (rows, M - 1))[..., first:first + cols]


def _t5_tables(t5_rel_bias):
    T, TK = DA_TQ, DA_TK
    base = T - 1 - DA_D_LO * TK
    first = LANES - 1
    assert (base - first) % LANES == 0
    M = base + DA_D_HI * TK + TK + 8
    rel = jnp.arange(M, dtype=jnp.int32) - base
    vec = jnp.transpose(t5_rel_bias[_t5_bucket(rel)].astype(jnp.float32) * LOG2E)
    band = _toeplitz(vec, LANES, first, M - 1 - first)
    tiles = [jnp.concatenate(
        [band[:, :, base - first + d * TK - a * LANES:][:, :, :TK] for a in range(T // LANES)], axis=1)
        for d in range(DA_D_LO, DA_D_HI + 1)]
    return jnp.stack(tiles, axis=1)


def _na_tables(na_rpb):
    n_dr, n_dc = 2 * NA_ROWS - 1, 2 * NA_COLS - 1
    c = jnp.arange(GRID_W, dtype=jnp.int32)[:, None]
    kc = jnp.arange(GRID_W, dtype=jnp.int32)[None, :]
    cs = jnp.clip(c - NA_COLS // 2, 0, GRID_W - NA_COLS)
    rpb = na_rpb.astype(jnp.float32)[:, :, :, None, None] * LOG2E
    band = jnp.zeros((NA_HEADS, n_dr, GRID_W, GRID_W), jnp.float32)
    for dc in range(n_dc):
        band = jnp.where(kc - c + (NA_COLS - 1) == dc, rpb[:, :, dc], band)
    band = jnp.where((kc >= cs) & (kc < cs + NA_COLS), band, NEG_BIG)
    per_var = [jnp.transpose(band[:, NA_ROWS - 1 - v:n_dr - v], (0, 2, 1, 3)) for v in range(NA_ROWS)]
    return jnp.stack(per_var, axis=0).reshape(NA_ROWS, NA_HEADS // NA_GROUP, NA_GROUP * GRID_W,
                                              NA_ROWS * GRID_W)


def _layer(x, tables, pre_w, post_w, w_in, lam, subln_w, w_o_diff, w_o_na, w_out, lam_init):
    B, S, D = x.shape
    t5_tiles, na_bias = tables
    x2d = x.reshape(B * S, D)
    proj2d = _proj(x2d, pre_w, w_in)
    proj = proj2d.reshape(B, S, IN_WIDTH)
    oa = _diff_attention(proj, lam, t5_tiles, subln_w, lam_init)
    on = _neighbourhood_attention(proj, na_bias)
    y = _output(x2d, oa.reshape(B * S, DA_WIDTH), on.reshape(B * S, NA_WIDTH), proj2d,
                w_o_diff, w_o_na, w_out, post_w)
    return y.reshape(B, S, D)


def _trunk(xs, t5_rel_bias, pre_norm_w, post_norm_w, w_in, lambda_q1, lambda_k1, lambda_q2,
           lambda_k2, subln_w, na_rpb, w_o_diff, w_o_na, w_out):
    depth = w_in.shape[0]
    t5_tiles = _t5_tables(t5_rel_bias)
    f32 = jnp.float32
    for l in range(depth):
        lam_init = 0.8 - 0.6 * math.exp(-0.3 * l)
        lam = (jnp.exp(jnp.sum(lambda_q1[l].astype(f32) * lambda_k1[l].astype(f32)))
               - jnp.exp(jnp.sum(lambda_q2[l].astype(f32) * lambda_k2[l].astype(f32)))
               + lam_init).reshape(1)
        tables = (t5_tiles, _na_tables(na_rpb[l]))
        args = (pre_norm_w[l].reshape(1, D_MODEL).astype(f32),
                post_norm_w[l].reshape(1, D_MODEL).astype(f32),
                w_in[l].astype(jnp.bfloat16), lam,
                subln_w[l].reshape(1, DA_V_DIM).astype(f32),
                w_o_diff[l].astype(jnp.bfloat16), w_o_na[l].astype(jnp.bfloat16),
                w_out[l].astype(jnp.bfloat16), lam_init)
        xs = [_layer(x, tables, *args) for x in xs]
    return xs


def kernel(x_prompt, x_sample, t5_rel_bias, pre_norm_w, post_norm_w, w_in, lambda_q1, lambda_k1,
           lambda_q2, lambda_k2, subln_w, na_rpb, w_o_diff, w_o_na, w_out):
    y_prompt, y_sample = _trunk([x_prompt, x_sample], t5_rel_bias, pre_norm_w, post_norm_w, w_in,
                                lambda_q1, lambda_k1, lambda_q2, lambda_k2, subln_w, na_rpb,
                                w_o_diff, w_o_na, w_out)
    return (y_prompt, y_sample)
```

```python
import functools
import math

import jax
import jax.numpy as jnp
from jax import lax
from jax.experimental import pallas as pl
from jax.experimental.pallas import tpu as pltpu

D_MODEL = 1024
DA_HEADS = 4
DA_HEAD_DIM = 64
DA_V_DIM = 2 * DA_HEAD_DIM
DA_WIDTH = DA_HEADS * DA_V_DIM
T5_BUCKETS = 32
T5_MAX_DIST = 128
GRID_W = 64
NA_HEADS = 8
NA_HEAD_DIM = 64
NA_WIDTH = NA_HEADS * NA_HEAD_DIM
NA_ROWS = 8
NA_COLS = 16
IN_WIDTH = 4 * DA_WIDTH + 4 * NA_WIDTH + 2 * D_MODEL
NORM_EPS = 1e-6
SUBLN_EPS = 1e-5
LOG2E = math.log2(math.e)

LANES = 128
VMEM_LIMIT_BYTES = 56 * 1024 * 1024

PROJ_TM = 512
PROJ_CHUNK = 512
DA_TQ = 512
DA_TK = 256
DA_D_LO = -((T5_MAX_DIST + DA_TK - 1 + DA_TK - 1) // DA_TK)
DA_D_HI = (T5_MAX_DIST + DA_TQ - 1 + DA_TK - 1) // DA_TK
DA_BIAS_TILES = DA_D_HI - DA_D_LO + 1
NA_RB = 16
NA_GROUP = 4
OUT_TM = 512
NEG_BIG = -1e30

_QA, _KA, _VA = 0, DA_HEADS, 2 * DA_HEADS
_ZA_BLK, _QN_BLK, _KN_BLK, _VN_BLK, _ZN_BLK = 3, 4, 5, 6, 7
_G_BLK = 2

_NT = (((1,), (1,)), ((), ()))


def _params(sem):
    return pltpu.CompilerParams(dimension_semantics=sem, vmem_limit_bytes=VMEM_LIMIT_BYTES)


def _proj_kernel(x_ref, pw_ref, w_ref, o_ref, *, scales):
    x = x_ref[...]
    ms = jnp.mean(x * x, axis=-1, keepdims=True)
    h = (x * lax.rsqrt(ms + NORM_EPS) * pw_ref[...]).astype(jnp.bfloat16)
    for c, sc in enumerate(scales):
        cols = slice(c * PROJ_CHUNK, (c + 1) * PROJ_CHUNK)
        acc = jnp.dot(h, w_ref[:, cols], preferred_element_type=jnp.float32)
        if sc != 1.0:
            acc = acc * sc
        o_ref[:, cols] = acc.astype(jnp.bfloat16)


def _proj(x2d, pre_w, w_in_bf16):
    tokens = x2d.shape[0]
    assert tokens % PROJ_TM == 0
    qscale = LOG2E * DA_HEAD_DIM ** -0.5
    assert DA_HEAD_DIM == NA_HEAD_DIM
    scales = [1.0] * (IN_WIDTH // PROJ_CHUNK)
    scales[0] = qscale
    scales[(4 * DA_WIDTH) // PROJ_CHUNK] = qscale
    assert DA_WIDTH == PROJ_CHUNK and NA_WIDTH == PROJ_CHUNK
    return pl.pallas_call(
        functools.partial(_proj_kernel, scales=tuple(scales)),
        grid=(tokens // PROJ_TM,),
        in_specs=[
            pl.BlockSpec((PROJ_TM, D_MODEL), lambda i: (i, 0)),
            pl.BlockSpec((1, D_MODEL), lambda i: (0, 0)),
            pl.BlockSpec((D_MODEL, IN_WIDTH), lambda i: (0, 0)),
        ],
        out_specs=pl.BlockSpec((PROJ_TM, IN_WIDTH), lambda i: (i, 0)),
        out_shape=jax.ShapeDtypeStruct((tokens, IN_WIDTH), jnp.bfloat16),
        compiler_params=_params(("parallel",)),
        name="proj",
    )(x2d, pre_w, w_in_bf16)


def _lane_groups(x):
    return [x[:, g * LANES:(g + 1) * LANES] for g in range(x.shape[1] // LANES)]


def _da_kernel(lam_ref, q_ref, k_ref, v_ref, bias_ref, sw_ref, o_ref,
               s_scr, q2_scr, mx_scr, m_scr, l_scr, acc_scr, *, n_chunks, n_blocks, out_scale):
    T = DA_TQ
    TK = DA_TK
    t = pl.program_id(0)

    @pl.when(t == 0)
    def _():
        s_scr[...] = jnp.zeros(s_scr.shape, jnp.float32)
        m_scr[...] = jnp.zeros(m_scr.shape, jnp.float32)
        l_scr[...] = jnp.ones(l_scr.shape, jnp.float32)
        acc_scr[...] = jnp.zeros(acc_scr.shape, jnp.float32)

    c0 = lax.rem(jnp.minimum(t, n_blocks - 1), n_chunks * TK // T) * (T // TK)
    lane = lax.broadcasted_iota(jnp.int32, (T, LANES), 1)

    def lane_tile(x):
        return jnp.concatenate([x] * (TK // LANES), axis=1)

    def group_reduce(x, op):
        groups = _lane_groups(x)
        out = groups[0]
        for g in groups[1:]:
            out = op(out, g)
        return out

    def step(cur):
        prv = 1 - cur
        q = q_ref[0]
        zero = jnp.zeros_like(q)
        q2_scr[:T] = jnp.where(lane < DA_HEAD_DIM, q, zero)
        q2_scr[T:] = jnp.where(lane >= DA_HEAD_DIM, q, zero)
        mx_scr[...] = jnp.full(mx_scr.shape, -jnp.inf, jnp.float32)

        l0 = jnp.sum(l_scr[cur, 0], axis=-1, keepdims=True)
        l1 = jnp.sum(l_scr[cur, 1], axis=-1, keepdims=True)
        o = acc_scr[:T] / l0 - lam_ref[0] * (acc_scr[T:] / l1)
        ms = jnp.mean(o * o, axis=-1, keepdims=True)
        o = o * lax.rsqrt(ms + SUBLN_EPS) * sw_ref[...] * out_scale
        o_ref[0] = o.astype(jnp.bfloat16)
        l_scr[prv] = jnp.zeros((2, T, LANES), jnp.float32)

        def stage1(c):
            bias = bias_ref[0, jnp.clip(c - c0, DA_D_LO, DA_D_HI) - DA_D_LO]
            s2 = lax.dot_general(q2_scr[...], k_ref[0, c * TK:(c + 1) * TK, :], _NT,
                                 preferred_element_type=jnp.float32)
            for m in range(2):
                s = s2[m * T:(m + 1) * T] + bias
                s_scr[cur, m, c] = s
                mx_scr[m] = jnp.maximum(mx_scr[m], group_reduce(s, jnp.maximum))

        def stage2(c, acc):
            ps = []
            for m in range(2):
                p = jnp.exp2(s_scr[prv, m, c] - lane_tile(m_scr[prv, m]))
                l_scr[prv, m] = l_scr[prv, m] + group_reduce(p, jnp.add)
                ps.append(p.astype(jnp.bfloat16))
            return acc + jnp.dot(jnp.concatenate(ps, axis=0), v_ref[0, c * TK:(c + 1) * TK, :],
                                 preferred_element_type=jnp.float32)

        acc = jnp.zeros((2 * T, DA_V_DIM), jnp.float32)
        for c in range(n_chunks):
            stage1(c)
            acc = stage2(c, acc)
        for m in range(2):
            row_max = jnp.max(mx_scr[m], axis=-1, keepdims=True)
            m_scr[cur, m] = jnp.broadcast_to(row_max, (T, LANES))
        acc_scr[...] = acc

    parity = lax.rem(t, 2)
    for cur in range(2):
        pl.when(parity == cur)(functools.partial(step, cur))


def _diff_attention(proj, lam, bias_tiles, subln_w, lam_init):
    B, S, _ = proj.shape
    T, TK = DA_TQ, DA_TK
    assert S % T == 0 and T % TK == 0
    n = S // TK
    nq = S // T
    n_blocks = B * DA_HEADS * nq

    def block(t, lag):
        u = jnp.clip(t - lag, 0, n_blocks - 1)
        return lax.div(u, DA_HEADS * nq), lax.rem(lax.div(u, nq), DA_HEADS), lax.rem(u, nq)

    def q_map(t):
        b, h, i = block(t, 0)
        return b, i, _QA + h

    def k_map(t):
        b, h, _ = block(t, 0)
        return b, 0, _KA + h

    def v_map(t):
        b, h, _ = block(t, 1)
        return b, 0, _VA + h

    def bias_map(t):
        _, h, _ = block(t, 0)
        return h, 0, 0, 0

    def o_map(t):
        b, h, i = block(t, 2)
        return b, i, h

    return pl.pallas_call(
        functools.partial(_da_kernel, n_chunks=n, n_blocks=n_blocks, out_scale=1.0 - lam_init),
        grid=(n_blocks + 2,),
        in_specs=[
            pl.BlockSpec(memory_space=pltpu.SMEM),
            pl.BlockSpec((1, T, LANES), q_map),
            pl.BlockSpec((1, S, LANES), k_map),
            pl.BlockSpec((1, S, LANES), v_map),
            pl.BlockSpec((1, DA_BIAS_TILES, T, TK), bias_map),
            pl.BlockSpec((1, DA_V_DIM), lambda t: (0, 0)),
        ],
        out_specs=pl.BlockSpec((1, T, LANES), o_map),
        out_shape=jax.ShapeDtypeStruct((B, S, DA_WIDTH), jnp.bfloat16),
        scratch_shapes=[
            pltpu.VMEM((2, 2, n, T, TK), jnp.float32),
            pltpu.VMEM((2 * T, LANES), jnp.bfloat16),
            pltpu.VMEM((2, T, LANES), jnp.float32),
            pltpu.VMEM((2, 2, T, LANES), jnp.float32),
            pltpu.VMEM((2, 2, T, LANES), jnp.float32),
            pltpu.VMEM((2 * T, DA_V_DIM), jnp.float32),
        ],
        compiler_params=_params(("arbitrary",)),
        name="diff_attn",
    )(lam, proj, proj, proj, bias_tiles, subln_w)


def _na_kernel(q_ref, k_ref, v_ref, bias_ref, o_ref, *, rows):
    rb = pl.program_id(1)
    win = NA_ROWS * GRID_W
    gw = NA_GROUP * NA_HEAD_DIM
    head_of_lane = lax.broadcasted_iota(jnp.int32, (GRID_W, gw), 1) // NA_HEAD_DIM

    def row_body(i, carry):
        r = rb * NA_RB + i
        start = jnp.clip(r - NA_ROWS // 2, 0, rows - NA_ROWS)
        var = r - start
        krows = pl.ds(pl.multiple_of(start * GRID_W, GRID_W), win)
        qrows = pl.ds(pl.multiple_of(i * GRID_W, GRID_W), GRID_W)
        for g in range(NA_HEADS // NA_GROUP):
            cols = slice(g * gw, (g + 1) * gw)
            qg = q_ref[0, qrows, cols]
            zero = jnp.zeros_like(qg)
            qbd = jnp.concatenate([jnp.where(head_of_lane == h, qg, zero) for h in range(NA_GROUP)],
                                  axis=0)
            s = lax.dot_general(qbd, k_ref[0, krows, cols], _NT, preferred_element_type=jnp.float32)
            s = s + bias_ref[var, g]
            m = jnp.max(s, axis=-1, keepdims=True)
            p = jnp.exp2(s - m)
            l = jnp.sum(p, axis=-1, keepdims=True)
            pv = jnp.dot(p.astype(jnp.bfloat16), v_ref[0, krows, cols],
                         preferred_element_type=jnp.float32) / l
            o = pv[:GRID_W]
            for h in range(1, NA_GROUP):
                o = jnp.where(head_of_lane == h, pv[h * GRID_W:(h + 1) * GRID_W], o)
            o_ref[0, qrows, cols] = o.astype(jnp.bfloat16)
        return carry

    lax.fori_loop(0, NA_RB, row_body, 0, unroll=True)


def _neighbourhood_attention(proj, na_bias):
    B, S, _ = proj.shape
    rows = S // GRID_W
    assert S % GRID_W == 0 and rows >= NA_ROWS and rows % NA_RB == 0
    blk = NA_RB * GRID_W
    return pl.pallas_call(
        functools.partial(_na_kernel, rows=rows),
        grid=(B, rows // NA_RB),
        in_specs=[
            pl.BlockSpec((1, blk, NA_WIDTH), lambda b, i: (b, i, _QN_BLK)),
            pl.BlockSpec((1, S, NA_WIDTH), lambda b, i: (b, 0, _KN_BLK)),
            pl.BlockSpec((1, S, NA_WIDTH), lambda b, i: (b, 0, _VN_BLK)),
            pl.BlockSpec(na_bias.shape, lambda b, i: (0, 0, 0, 0)),
        ],
        out_specs=pl.BlockSpec((1, blk, NA_WIDTH), lambda b, i: (b, i, 0)),
        out_shape=jax.ShapeDtypeStruct((B, S, NA_WIDTH), jnp.bfloat16),
        compiler_params=_params(("parallel", "parallel")),
        name="nbr_attn",
    )(proj, proj, proj, na_bias)


def _out_kernel(x_ref, oa_ref, on_ref, za_ref, zn_ref, g_ref, woa_ref, won_ref, wout_ref,
                pw_ref, y_ref):
    def sigmoid(v):
        return 0.5 * jnp.tanh(0.5 * v) + 0.5

    def branch(o_ref, z_ref, w_ref):
        h = 0.5 * z_ref[...].astype(jnp.float32)
        u = o_ref[...].astype(jnp.float32) * (h + h * jnp.tanh(h))
        return jnp.dot(u.astype(jnp.bfloat16), w_ref[...], preferred_element_type=jnp.float32)

    ya = branch(oa_ref, za_ref, woa_ref)
    yn = branch(on_ref, zn_ref, won_ref)
    ga = sigmoid(g_ref[:, :D_MODEL].astype(jnp.float32))
    gn = sigmoid(g_ref[:, D_MODEL:].astype(jnp.float32))
    merged = (ga * ya + gn * yn).astype(jnp.bfloat16)
    out = jnp.dot(merged, wout_ref[...], preferred_element_type=jnp.float32)
    ms = jnp.mean(out * out, axis=-1, keepdims=True)
    y_ref[...] = x_ref[...] + out * lax.rsqrt(ms + NORM_EPS) * pw_ref[...]


def _output(x2d, oa2d, on2d, proj2d, w_o_diff, w_o_na, w_out, post_w):
    tokens = x2d.shape[0]
    tm = OUT_TM
    assert tokens % tm == 0

    def const(shape):
        return pl.BlockSpec(shape, lambda i: (0, 0))

    return pl.pallas_call(
        _out_kernel,
        grid=(tokens // tm,),
        in_specs=[
            pl.BlockSpec((tm, D_MODEL), lambda i: (i, 0)),
            pl.BlockSpec((tm, DA_WIDTH), lambda i: (i, 0)),
            pl.BlockSpec((tm, NA_WIDTH), lambda i: (i, 0)),
            pl.BlockSpec((tm, DA_WIDTH), lambda i: (i, _ZA_BLK)),
            pl.BlockSpec((tm, NA_WIDTH), lambda i: (i, _ZN_BLK)),
            pl.BlockSpec((tm, 2 * D_MODEL), lambda i: (i, _G_BLK)),
            const((DA_WIDTH, D_MODEL)),
            const((NA_WIDTH, D_MODEL)),
            const((D_MODEL, D_MODEL)),
            const((1, D_MODEL)),
        ],
        out_specs=pl.BlockSpec((tm, D_MODEL), lambda i: (i, 0)),
        out_shape=jax.ShapeDtypeStruct((tokens, D_MODEL), jnp.float32),
        compiler_params=_params(("parallel",)),
        name="out_proj",
    )(x2d, oa2d, on2d, proj2d, proj2d, proj2d, w_o_diff, w_o_na, w_out, post_w)


def _t5_bucket(rel):
    nb = T5_BUCKETS // 2
    ret = jnp.where(rel > 0, nb, 0)
    n = jnp.abs(rel)
    max_exact = nb // 2
    nf = jnp.maximum(n, 1).astype(jnp.float32)
    large = max_exact + (jnp.log(nf / max_exact) / math.log(T5_MAX_DIST / max_exact)
                         * (nb - max_exact)).astype(jnp.int32)
    large = jnp.minimum(large, nb - 1)
    return ret + jnp.where(n < max_exact, n, large)


def _toeplitz(w, rows, first, cols):
    M = w.shape[-1]
    assert first - (rows - 1) >= 0 and first + cols <= M - 1
    lead = w.shape[:-1]
    flat = jnp.broadcast_to(w[..., None, :], lead + (rows, M)).reshape(lead + (rows * M,))
    return flat[..., :rows * (M - 1)].reshape(lead + (rows, M - 1))[..., first:first + cols]


def _t5_tables(t5_rel_bias):
    T, TK = DA_TQ, DA_TK
    base = T - 1 - DA_D_LO * TK
    first = LANES - 1
    assert (base - first) % LANES == 0
    M = base + DA_D_HI * TK + TK + 8
    rel = jnp.arange(M, dtype=jnp.int32) - base
    vec = jnp.transpose(t5_rel_bias[_t5_bucket(rel)].astype(jnp.float32) * LOG2E)
    band = _toeplitz(vec, LANES, first, M - 1 - first)
    tiles = [jnp.concatenate(
        [band[:, :, base - first + d * TK - a * LANES:][:, :, :TK] for a in range(T // LANES)], axis=1)
        for d in range(DA_D_LO, DA_D_HI + 1)]
    return jnp.stack(tiles, axis=1)


def _na_tables(na_rpb):
    n_dr, n_dc = 2 * NA_ROWS - 1, 2 * NA_COLS - 1
    c = jnp.arange(GRID_W, dtype=jnp.int32)[:, None]
    kc = jnp.arange(GRID_W, dtype=jnp.int32)[None, :]
    cs = jnp.clip(c - NA_COLS // 2, 0, GRID_W - NA_COLS)
    rpb = na_rpb.astype(jnp.float32)[:, :, :, None, None] * LOG2E
    band = jnp.zeros((NA_HEADS, n_dr, GRID_W, GRID_W), jnp.float32)
    for dc in range(n_dc):
        band = jnp.where(kc - c + (NA_COLS - 1) == dc, rpb[:, :, dc], band)
    band = jnp.where((kc >= cs) & (kc < cs + NA_COLS), band, NEG_BIG)
    per_var = [jnp.transpose(band[:, NA_ROWS - 1 - v:n_dr - v], (0, 2, 1, 3)) for v in range(NA_ROWS)]
    return jnp.stack(per_var, axis=0).reshape(NA_ROWS, NA_HEADS // NA_GROUP, NA_GROUP * GRID_W,
                                              NA_ROWS * GRID_W)


def _layer(x, tables, pre_w, post_w, w_in, lam, subln_w, w_o_diff, w_o_na, w_out, lam_init):
    B, S, D = x.shape
    t5_tiles, na_bias = tables
    x2d = x.reshape(B * S, D)
    proj2d = _proj(x2d, pre_w, w_in)
    proj = proj2d.reshape(B, S, IN_WIDTH)
    oa = _diff_attention(proj, lam, t5_tiles, subln_w, lam_init)
    on = _neighbourhood_attention(proj, na_bias)
    y = _output(x2d, oa.reshape(B * S, DA_WIDTH), on.reshape(B * S, NA_WIDTH), proj2d,
                w_o_diff, w_o_na, w_out, post_w)
    return y.reshape(B, S, D)


def _trunk(xs, t5_rel_bias, pre_norm_w, post_norm_w, w_in, lambda_q1, lambda_k1, lambda_q2,
           lambda_k2, subln_w, na_rpb, w_o_diff, w_o_na, w_out):
    depth = w_in.shape[0]
    t5_tiles = _t5_tables(t5_rel_bias)
    f32 = jnp.float32
    for l in range(depth):
        lam_init = 0.8 - 0.6 * math.exp(-0.3 * l)
        lam = (jnp.exp(jnp.sum(lambda_q1[l].astype(f32) * lambda_k1[l].astype(f32)))
               - jnp.exp(jnp.sum(lambda_q2[l].astype(f32) * lambda_k2[l].astype(f32)))
               + lam_init).reshape(1)
        tables = (t5_tiles, _na_tables(na_rpb[l]))
        args = (pre_norm_w[l].reshape(1, D_MODEL).astype(f32),
                post_norm_w[l].reshape(1, D_MODEL).astype(f32),
                w_in[l].astype(jnp.bfloat16), lam,
                subln_w[l].reshape(1, DA_V_DIM).astype(f32),
                w_o_diff[l].astype(jnp.bfloat16), w_o_na[l].astype(jnp.bfloat16),
                w_out[l].astype(jnp.bfloat16), lam_init)
        xs = [_layer(x, tables, *args) for x in xs]
    return xs


def kernel(x_prompt, x_sample, t5_rel_bias, pre_norm_w, post_norm_w, w_in, lambda_q1, lambda_k1,
           lambda_q2, lambda_k2, subln_w, na_rpb, w_o_diff, w_o_na, w_out):
    y_prompt, y_sample = _trunk([x_prompt, x_sample], t5_rel_bias, pre_norm_w, post_norm_w, w_in,
                                lambda_q1, lambda_k1, lambda_q2, lambda_k2, subln_w, na_rpb,
                                w_o_diff, w_o_na, w_out)
    return (y_prompt, y_sample)
```

```python
import functools
import math

import jax
import jax.numpy as jnp
from jax import lax
from jax.experimental import pallas as pl
from jax.experimental.pallas import tpu as pltpu

D_MODEL = 1024
DA_HEADS = 4
DA_HEAD_DIM = 64
DA_V_DIM = 2 * DA_HEAD_DIM
DA_WIDTH = DA_HEADS * DA_V_DIM
T5_BUCKETS = 32
T5_MAX_DIST = 128
GRID_W = 64
NA_HEADS = 8
NA_HEAD_DIM = 64
NA_WIDTH = NA_HEADS * NA_HEAD_DIM
NA_ROWS = 8
NA_COLS = 16
IN_WIDTH = 4 * DA_WIDTH + 4 * NA_WIDTH + 2 * D_MODEL
NORM_EPS = 1e-6
SUBLN_EPS = 1e-5
LOG2E = math.log2(math.e)

LANES = 128
VMEM_LIMIT_BYTES = 56 * 1024 * 1024

PROJ_TM = 512
PROJ_CHUNK = 512
DA_TQ = 512
DA_TK = 256
DA_D_LO = -((T5_MAX_DIST + DA_TK - 1 + DA_TK - 1) // DA_TK)
DA_D_HI = (T5_MAX_DIST + DA_TQ - 1 + DA_TK - 1) // DA_TK
DA_BIAS_TILES = DA_D_HI - DA_D_LO + 1
NA_RB = 16
NA_GROUP = 4
OUT_TM = 512
NEG_BIG = -1e30

_QA, _KA, _VA, _ZA = 0, DA_HEADS, 2 * DA_HEADS, 3 * DA_HEADS
_QN_BLK, _KN_BLK, _VN_BLK, _ZN_BLK = 4, 5, 6, 7
_G_BLK = 2

_NT = (((1,), (1,)), ((), ()))


def _params(sem):
    return pltpu.CompilerParams(dimension_semantics=sem, vmem_limit_bytes=VMEM_LIMIT_BYTES)


def _proj_kernel(x_ref, pw_ref, w_ref, o_ref, *, scales):
    x = x_ref[...]
    ms = jnp.mean(x * x, axis=-1, keepdims=True)
    h = (x * lax.rsqrt(ms + NORM_EPS) * pw_ref[...]).astype(jnp.bfloat16)
    for c, sc in enumerate(scales):
        cols = slice(c * PROJ_CHUNK, (c + 1) * PROJ_CHUNK)
        acc = jnp.dot(h, w_ref[:, cols], preferred_element_type=jnp.float32)
        if sc != 1.0:
            acc = acc * sc
        o_ref[:, cols] = acc.astype(jnp.bfloat16)


def _proj(x2d, pre_w, w_in_bf16):
    tokens = x2d.shape[0]
    assert tokens % PROJ_TM == 0
    qscale = LOG2E * DA_HEAD_DIM ** -0.5
    assert DA_HEAD_DIM == NA_HEAD_DIM
    scales = [1.0] * (IN_WIDTH // PROJ_CHUNK)
    scales[0] = qscale
    scales[(4 * DA_WIDTH) // PROJ_CHUNK] = qscale
    assert DA_WIDTH == PROJ_CHUNK and NA_WIDTH == PROJ_CHUNK
    return pl.pallas_call(
        functools.partial(_proj_kernel, scales=tuple(scales)),
        grid=(tokens // PROJ_TM,),
        in_specs=[
            pl.BlockSpec((PROJ_TM, D_MODEL), lambda i: (i, 0)),
            pl.BlockSpec((1, D_MODEL), lambda i: (0, 0)),
            pl.BlockSpec((D_MODEL, IN_WIDTH), lambda i: (0, 0)),
        ],
        out_specs=pl.BlockSpec((PROJ_TM, IN_WIDTH), lambda i: (i, 0)),
        out_shape=jax.ShapeDtypeStruct((tokens, IN_WIDTH), jnp.bfloat16),
        compiler_params=_params(("parallel",)),
        name="proj",
    )(x2d, pre_w, w_in_bf16)


def _silu(z):
    h = 0.5 * z.astype(jnp.float32)
    return h + h * jnp.tanh(h)


def _lane_groups(x):
    return [x[:, g * LANES:(g + 1) * LANES] for g in range(x.shape[1] // LANES)]


def _da_kernel(lam_ref, q_ref, k_ref, v_ref, bias_ref, sw_ref, z_ref, o_ref,
               s_scr, q2_scr, mx_scr, m_scr, l_scr, acc_scr, *, n_chunks, n_blocks, out_scale):
    T = DA_TQ
    TK = DA_TK
    t = pl.program_id(0)

    @pl.when(t == 0)
    def _():
        s_scr[...] = jnp.zeros(s_scr.shape, jnp.float32)
        m_scr[...] = jnp.zeros(m_scr.shape, jnp.float32)
        l_scr[...] = jnp.ones(l_scr.shape, jnp.float32)
        acc_scr[...] = jnp.zeros(acc_scr.shape, jnp.float32)

    c0 = lax.rem(jnp.minimum(t, n_blocks - 1), n_chunks * TK // T) * (T // TK)
    lane = lax.broadcasted_iota(jnp.int32, (T, LANES), 1)

    def lane_tile(x):
        return jnp.concatenate([x] * (TK // LANES), axis=1)

    def group_reduce(x, op):
        groups = _lane_groups(x)
        out = groups[0]
        for g in groups[1:]:
            out = op(out, g)
        return out

    def step(cur):
        prv = 1 - cur
        q = q_ref[0]
        zero = jnp.zeros_like(q)
        q2_scr[:T] = jnp.where(lane < DA_HEAD_DIM, q, zero)
        q2_scr[T:] = jnp.where(lane >= DA_HEAD_DIM, q, zero)
        mx_scr[...] = jnp.full(mx_scr.shape, -jnp.inf, jnp.float32)

        l0 = jnp.sum(l_scr[cur, 0], axis=-1, keepdims=True)
        l1 = jnp.sum(l_scr[cur, 1], axis=-1, keepdims=True)
        o = acc_scr[:T] / l0 - lam_ref[0] * (acc_scr[T:] / l1)
        ms = jnp.mean(o * o, axis=-1, keepdims=True)
        o = o * lax.rsqrt(ms + SUBLN_EPS) * sw_ref[...] * out_scale
        o_ref[0] = (o * _silu(z_ref[0])).astype(jnp.bfloat16)
        l_scr[prv] = jnp.zeros((2, T, LANES), jnp.float32)

        def stage1(c):
            bias = bias_ref[0, jnp.clip(c - c0, DA_D_LO, DA_D_HI) - DA_D_LO]
            s2 = lax.dot_general(q2_scr[...], k_ref[0, c * TK:(c + 1) * TK, :], _NT,
                                 preferred_element_type=jnp.float32)
            for m in range(2):
                s = s2[m * T:(m + 1) * T] + bias
                s_scr[cur, m, c] = s
                mx_scr[m] = jnp.maximum(mx_scr[m], group_reduce(s, jnp.maximum))

        def stage2(c, acc):
            ps = []
            for m in range(2):
                p = jnp.exp2(s_scr[prv, m, c] - lane_tile(m_scr[prv, m]))
                l_scr[prv, m] = l_scr[prv, m] + group_reduce(p, jnp.add)
                ps.append(p.astype(jnp.bfloat16))
            return acc + jnp.dot(jnp.concatenate(ps, axis=0), v_ref[0, c * TK:(c + 1) * TK, :],
                                 preferred_element_type=jnp.float32)

        acc = jnp.zeros((2 * T, DA_V_DIM), jnp.float32)
        for c in range(n_chunks):
            stage1(c)
            acc = stage2(c, acc)
        for m in range(2):
            row_max = jnp.max(mx_scr[m], axis=-1, keepdims=True)
            m_scr[cur, m] = jnp.broadcast_to(row_max, (T, LANES))
        acc_scr[...] = acc

    parity = lax.rem(t, 2)
    for cur in range(2):
        pl.when(parity == cur)(functools.partial(step, cur))


def _diff_attention(proj, lam, bias_tiles, subln_w, lam_init):
    B, S, _ = proj.shape
    T, TK = DA_TQ, DA_TK
    assert S % T == 0 and T % TK == 0
    n = S // TK
    nq = S // T
    n_blocks = B * DA_HEADS * nq

    def block(t, lag):
        u = jnp.clip(t - lag, 0, n_blocks - 1)
        return lax.div(u, DA_HEADS * nq), lax.rem(lax.div(u, nq), DA_HEADS), lax.rem(u, nq)

    def q_map(t):
        b, h, i = block(t, 0)
        return b, i, _QA + h

    def k_map(t):
        b, h, _ = block(t, 0)
        return b, 0, _KA + h

    def v_map(t):
        b, h, _ = block(t, 1)
        return b, 0, _VA + h

    def bias_map(t):
        _, h, _ = block(t, 0)
        return h, 0, 0, 0

    def z_map(t):
        b, h, i = block(t, 2)
        return b, i, _ZA + h

    def o_map(t):
        b, h, i = block(t, 2)
        return b, i, h

    return pl.pallas_call(
        functools.partial(_da_kernel, n_chunks=n, n_blocks=n_blocks, out_scale=1.0 - lam_init),
        grid=(n_blocks + 2,),
        in_specs=[
            pl.BlockSpec(memory_space=pltpu.SMEM),
            pl.BlockSpec((1, T, LANES), q_map),
            pl.BlockSpec((1, S, LANES), k_map),
            pl.BlockSpec((1, S, LANES), v_map),
            pl.BlockSpec((1, DA_BIAS_TILES, T, TK), bias_map),
            pl.BlockSpec((1, DA_V_DIM), lambda t: (0, 0)),
            pl.BlockSpec((1, T, LANES), z_map),
        ],
        out_specs=pl.BlockSpec((1, T, LANES), o_map),
        out_shape=jax.ShapeDtypeStruct((B, S, DA_WIDTH), jnp.bfloat16),
        scratch_shapes=[
            pltpu.VMEM((2, 2, n, T, TK), jnp.float32),
            pltpu.VMEM((2 * T, LANES), jnp.bfloat16),
            pltpu.VMEM((2, T, LANES), jnp.float32),
            pltpu.VMEM((2, 2, T, LANES), jnp.float32),
            pltpu.VMEM((2, 2, T, LANES), jnp.float32),
            pltpu.VMEM((2 * T, DA_V_DIM), jnp.float32),
        ],
        compiler_params=_params(("arbitrary",)),
        name="diff_attn",
    )(lam, proj, proj, proj, bias_tiles, subln_w, proj)


def _na_kernel(q_ref, k_ref, v_ref, bias_ref, z_ref, o_ref, *, rows):
    rb = pl.program_id(1)
    win = NA_ROWS * GRID_W
    gw = NA_GROUP * NA_HEAD_DIM
    head_of_lane = lax.broadcasted_iota(jnp.int32, (GRID_W, gw), 1) // NA_HEAD_DIM

    def row_body(i, carry):
        r = rb * NA_RB + i
        start = jnp.clip(r - NA_ROWS // 2, 0, rows - NA_ROWS)
        var = r - start
        krows = pl.ds(pl.multiple_of(start * GRID_W, GRID_W), win)
        qrows = pl.ds(pl.multiple_of(i * GRID_W, GRID_W), GRID_W)
        for g in range(NA_HEADS // NA_GROUP):
            cols = slice(g * gw, (g + 1) * gw)
            qg = q_ref[0, qrows, cols]
            zero = jnp.zeros_like(qg)
            qbd = jnp.concatenate([jnp.where(head_of_lane == h, qg, zero) for h in range(NA_GROUP)],
                                  axis=0)
            s = lax.dot_general(qbd, k_ref[0, krows, cols], _NT, preferred_element_type=jnp.float32)
            s = s + bias_ref[var, g]
            m = jnp.max(s, axis=-1, keepdims=True)
            p = jnp.exp2(s - m)
            l = jnp.sum(p, axis=-1, keepdims=True)
            pv = jnp.dot(p.astype(jnp.bfloat16), v_ref[0, krows, cols],
                         preferred_element_type=jnp.float32) / l
            o = pv[:GRID_W]
            for h in range(1, NA_GROUP):
                o = jnp.where(head_of_lane == h, pv[h * GRID_W:(h + 1) * GRID_W], o)
            o_ref[0, qrows, cols] = (o * _silu(z_ref[0, qrows, cols])).astype(jnp.bfloat16)
        return carry

    lax.fori_loop(0, NA_RB, row_body, 0, unroll=True)


def _neighbourhood_attention(proj, na_bias):
    B, S, _ = proj.shape
    rows = S // GRID_W
    assert S % GRID_W == 0 and rows >= NA_ROWS and rows % NA_RB == 0
    blk = NA_RB * GRID_W
    return pl.pallas_call(
        functools.partial(_na_kernel, rows=rows),
        grid=(B, rows // NA_RB),
        in_specs=[
            pl.BlockSpec((1, blk, NA_WIDTH), lambda b, i: (b, i, _QN_BLK)),
            pl.BlockSpec((1, S, NA_WIDTH), lambda b, i: (b, 0, _KN_BLK)),
            pl.BlockSpec((1, S, NA_WIDTH), lambda b, i: (b, 0, _VN_BLK)),
            pl.BlockSpec(na_bias.shape, lambda b, i: (0, 0, 0, 0)),
            pl.BlockSpec((1, blk, NA_WIDTH), lambda b, i: (b, i, _ZN_BLK)),
        ],
        out_specs=pl.BlockSpec((1, blk, NA_WIDTH), lambda b, i: (b, i, 0)),
        out_shape=jax.ShapeDtypeStruct((B, S, NA_WIDTH), jnp.bfloat16),
        compiler_params=_params(("parallel", "parallel")),
        name="nbr_attn",
    )(proj, proj, proj, na_bias, proj)


def _out_kernel(x_ref, ua_ref, un_ref, g_ref, woa_ref, won_ref, wout_ref, pw_ref, y_ref):
    def sigmoid(v):
        return 0.5 * jnp.tanh(0.5 * v) + 0.5

    ya = jnp.dot(ua_ref[...], woa_ref[...], preferred_element_type=jnp.float32)
    yn = jnp.dot(un_ref[...], won_ref[...], preferred_element_type=jnp.float32)
    ga = sigmoid(g_ref[:, :D_MODEL].astype(jnp.float32))
    gn = sigmoid(g_ref[:, D_MODEL:].astype(jnp.float32))
    merged = (ga * ya + gn * yn).astype(jnp.bfloat16)
    out = jnp.dot(merged, wout_ref[...], preferred_element_type=jnp.float32)
    ms = jnp.mean(out * out, axis=-1, keepdims=True)
    y_ref[...] = x_ref[...] + out * lax.rsqrt(ms + NORM_EPS) * pw_ref[...]


def _output(x2d, ua2d, un2d, proj2d, w_o_diff, w_o_na, w_out, post_w):
    tokens = x2d.shape[0]
    tm = OUT_TM
    assert tokens % tm == 0

    def const(shape):
        return pl.BlockSpec(shape, lambda i: (0, 0))

    return pl.pallas_call(
        _out_kernel,
        grid=(tokens // tm,),
        in_specs=[
            pl.BlockSpec((tm, D_MODEL), lambda i: (i, 0)),
            pl.BlockSpec((tm, DA_WIDTH), lambda i: (i, 0)),
            pl.BlockSpec((tm, NA_WIDTH), lambda i: (i, 0)),
            pl.BlockSpec((tm, 2 * D_MODEL), lambda i: (i, _G_BLK)),
            const((DA_WIDTH, D_MODEL)),
            const((NA_WIDTH, D_MODEL)),
            const((D_MODEL, D_MODEL)),
            const((1, D_MODEL)),
        ],
        out_specs=pl.BlockSpec((tm, D_MODEL), lambda i: (i, 0)),
        out_shape=jax.ShapeDtypeStruct((tokens, D_MODEL), jnp.float32),
        compiler_params=_params(("parallel",)),
        name="out_proj",
    )(x2d, ua2d, un2d, proj2d, w_o_diff, w_o_na, w_out, post_w)


def _t5_bucket(rel):
    nb = T5_BUCKETS // 2
    ret = jnp.where(rel > 0, nb, 0)
    n = jnp.abs(rel)
    max_exact = nb // 2
    nf = jnp.maximum(n, 1).astype(jnp.float32)
    large = max_exact + (jnp.log(nf / max_exact) / math.log(T5_MAX_DIST / max_exact)
                         * (nb - max_exact)).astype(jnp.int32)
    large = jnp.minimum(large, nb - 1)
    return ret + jnp.where(n < max_exact, n, large)


def _toeplitz(w, rows, first, cols):
    M = w.shape[-1]
    assert first - (rows - 1) >= 0 and first + cols <= M - 1
    lead = w.shape[:-1]
    flat = jnp.broadcast_to(w[..., None, :], lead + (rows, M)).reshape(lead + (rows * M,))
    return flat[..., :rows * (M - 1)].reshape(lead + (rows, M - 1))[..., first:first + cols]


def _t5_tables(t5_rel_bias):
    T, TK = DA_TQ, DA_TK
    base = T - 1 - DA_D_LO * TK
    first = LANES - 1
    assert (base - first) % LANES == 0
    M = base + DA_D_HI * TK + TK + 8
    rel = jnp.arange(M, dtype=jnp.int32) - base
    vec = jnp.transpose(t5_rel_bias[_t5_bucket(rel)].astype(jnp.float32) * LOG2E)
    band = _toeplitz(vec, LANES, first, M - 1 - first)
    tiles = [jnp.concatenate(
        [band[:, :, base - first + d * TK - a * LANES:][:, :, :TK] for a in range(T // LANES)], axis=1)
        for d in range(DA_D_LO, DA_D_HI + 1)]
    return jnp.stack(tiles, axis=1)


def _na_tables(na_rpb):
    n_dr, n_dc = 2 * NA_ROWS - 1, 2 * NA_COLS - 1
    c = jnp.arange(GRID_W, dtype=jnp.int32)[:, None]
    kc = jnp.arange(GRID_W, dtype=jnp.int32)[None, :]
    cs = jnp.clip(c - NA_COLS // 2, 0, GRID_W - NA_COLS)
    rpb = na_rpb.astype(jnp.float32)[:, :, :, None, None] * LOG2E
    band = jnp.zeros((NA_HEADS, n_dr, GRID_W, GRID_W), jnp.float32)
    for dc in range(n_dc):
        band = jnp.where(kc - c + (NA_COLS - 1) == dc, rpb[:, :, dc], band)
    band = jnp.where((kc >= cs) & (kc < cs + NA_COLS), band, NEG_BIG)
    per_var = [jnp.transpose(band[:, NA_ROWS - 1 - v:n_dr - v], (0, 2, 1, 3)) for v in range(NA_ROWS)]
    return jnp.stack(per_var, axis=0).reshape(NA_ROWS, NA_HEADS // NA_GROUP, NA_GROUP * GRID_W,
                                              NA_ROWS * GRID_W)


def _layer(x, tables, pre_w, post_w, w_in, lam, subln_w, w_o_diff, w_o_na, w_out, lam_init):
    B, S, D = x.shape
    t5_tiles, na_bias = tables
    x2d = x.reshape(B * S, D)
    proj2d = _proj(x2d, pre_w, w_in)
    proj = proj2d.reshape(B, S, IN_WIDTH)
    oa = _diff_attention(proj, lam, t5_tiles, subln_w, lam_init)
    on = _neighbourhood_attention(proj, na_bias)
    y = _output(x2d, oa.reshape(B * S, DA_WIDTH), on.reshape(B * S, NA_WIDTH), proj2d,
                w_o_diff, w_o_na, w_out, post_w)
    return y.reshape(B, S, D)


def _trunk(xs, t5_rel_bias, pre_norm_w, post_norm_w, w_in, lambda_q1, lambda_k1, lambda_q2,
           lambda_k2, subln_w, na_rpb, w_o_diff, w_o_na, w_out):
    depth = w_in.shape[0]
    t5_tiles = _t5_tables(t5_rel_bias)
    f32 = jnp.float32
    for l in range(depth):
        lam_init = 0.8 - 0.6 * math.exp(-0.3 * l)
        lam = (jnp.exp(jnp.sum(lambda_q1[l].astype(f32) * lambda_k1[l].astype(f32)))
               - jnp.exp(jnp.sum(lambda_q2[l].astype(f32) * lambda_k2[l].astype(f32)))
               + lam_init).reshape(1)
        tables = (t5_tiles, _na_tables(na_rpb[l]))
        args = (pre_norm_w[l].reshape(1, D_MODEL).astype(f32),
                post_norm_w[l].reshape(1, D_MODEL).astype(f32),
                w_in[l].astype(jnp.bfloat16), lam,
                subln_w[l].reshape(1, DA_V_DIM).astype(f32),
                w_o_diff[l].astype(jnp.bfloat16), w_o_na[l].astype(jnp.bfloat16),
                w_out[l].astype(jnp.bfloat16), lam_init)
        xs = [_layer(x, tables, *args) for x in xs]
    return xs


def kernel(x_prompt, x_sample, t5_rel_bias, pre_norm_w, post_norm_w, w_in, lambda_q1, lambda_k1,
           lambda_q2, lambda_k2, subln_w, na_rpb, w_o_diff, w_o_na, w_out):
    y_prompt, y_sample = _trunk([x_prompt, x_sample], t5_rel_bias, pre_norm_w, post_norm_w, w_in,
                                lambda_q1, lambda_k1, lambda_q2, lambda_k2, subln_w, na_rpb,
                                w_o_diff, w_o_na, w_out)
    return (y_prompt, y_sample)
```

```python
import functools
import math

import jax
import jax.numpy as jnp
from jax import lax
from jax.experimental import pallas as pl
from jax.experimental.pallas import tpu as pltpu

D_MODEL = 1024
DA_HEADS = 4
DA_HEAD_DIM = 64
DA_V_DIM = 2 * DA_HEAD_DIM
DA_WIDTH = DA_HEADS * DA_V_DIM
T5_BUCKETS = 32
T5_MAX_DIST = 128
GRID_W = 64
NA_HEADS = 8
NA_HEAD_DIM = 64
NA_WIDTH = NA_HEADS * NA_HEAD_DIM
NA_ROWS = 8
NA_COLS = 16
IN_WIDTH = 4 * DA_WIDTH + 4 * NA_WIDTH + 2 * D_MODEL
NORM_EPS = 1e-6
SUBLN_EPS = 1e-5
LOG2E = math.log2(math.e)

LANES = 128
VMEM_LIMIT_BYTES = 56 * 1024 * 1024

PROJ_TM = 512
PROJ_CHUNK = 512
DA_TQ = 512
DA_TK = 256
DA_D_LO = -((T5_MAX_DIST + DA_TK - 1 + DA_TK - 1) // DA_TK)
DA_D_HI = (T5_MAX_DIST + DA_TQ - 1 + DA_TK - 1) // DA_TK
DA_BIAS_TILES = DA_D_HI - DA_D_LO + 1
NA_RB = 16
NA_GROUP = 4
OUT_TM = 512
NEG_BIG = -1e30

_QA, _KA, _VA, _ZA = 0, DA_HEADS, 2 * DA_HEADS, 3 * DA_HEADS
_QN_BLK, _KN_BLK, _VN_BLK, _ZN_BLK = 4, 5, 6, 7
_G_BLK = 2

_NT = (((1,), (1,)), ((), ()))


def _params(sem):
    return pltpu.CompilerParams(dimension_semantics=sem, vmem_limit_bytes=VMEM_LIMIT_BYTES)


def _proj_kernel(x_ref, pw_ref, w_ref, o_ref, *, scales):
    x = x_ref[...]
    ms = jnp.mean(x * x, axis=-1, keepdims=True)
    h = (x * lax.rsqrt(ms + NORM_EPS) * pw_ref[...]).astype(jnp.bfloat16)
    for c, sc in enumerate(scales):
        cols = slice(c * PROJ_CHUNK, (c + 1) * PROJ_CHUNK)
        acc = jnp.dot(h, w_ref[:, cols], preferred_element_type=jnp.float32)
        if sc != 1.0:
            acc = acc * sc
        o_ref[:, cols] = acc.astype(jnp.bfloat16)


def _proj(x2d, pre_w, w_in_bf16):
    tokens = x2d.shape[0]
    assert tokens % PROJ_TM == 0
    qscale = LOG2E * DA_HEAD_DIM ** -0.5
    assert DA_HEAD_DIM == NA_HEAD_DIM
    scales = [1.0] * (IN_WIDTH // PROJ_CHUNK)
    scales[0] = qscale
    scales[(4 * DA_WIDTH) // PROJ_CHUNK] = qscale
    assert DA_WIDTH == PROJ_CHUNK and NA_WIDTH == PROJ_CHUNK
    return pl.pallas_call(
        functools.partial(_proj_kernel, scales=tuple(scales)),
        grid=(tokens // PROJ_TM,),
        in_specs=[
            pl.BlockSpec((PROJ_TM, D_MODEL), lambda i: (i, 0)),
            pl.BlockSpec((1, D_MODEL), lambda i: (0, 0)),
            pl.BlockSpec((D_MODEL, IN_WIDTH), lambda i: (0, 0)),
        ],
        out_specs=pl.BlockSpec((PROJ_TM, IN_WIDTH), lambda i: (i, 0)),
        out_shape=jax.ShapeDtypeStruct((tokens, IN_WIDTH), jnp.bfloat16),
        compiler_params=_params(("parallel",)),
        name="proj",
    )(x2d, pre_w, w_in_bf16)


def _silu(z):
    h = 0.5 * z.astype(jnp.float32)
    return h + h * jnp.tanh(h)


def _lane_groups(x):
    return [x[:, g * LANES:(g + 1) * LANES] for g in range(x.shape[1] // LANES)]


def _da_kernel(lam_ref, q_ref, k_ref, v_ref, bias_ref, sw_ref, z_ref, o_ref,
               s_scr, q2_scr, mx_scr, m_scr, l_scr, acc_scr, *, n_chunks, n_blocks, out_scale):
    T = DA_TQ
    TK = DA_TK
    t = pl.program_id(0)

    @pl.when(t == 0)
    def _():
        s_scr[...] = jnp.zeros(s_scr.shape, jnp.float32)
        m_scr[...] = jnp.zeros(m_scr.shape, jnp.float32)
        l_scr[...] = jnp.ones(l_scr.shape, jnp.float32)
        acc_scr[...] = jnp.zeros(acc_scr.shape, jnp.float32)

    c0 = lax.rem(jnp.minimum(t, n_blocks - 1), n_chunks * TK // T) * (T // TK)
    lane = lax.broadcasted_iota(jnp.int32, (T, LANES), 1)

    def lane_tile(x):
        return jnp.concatenate([x] * (TK // LANES), axis=1)

    def group_reduce(x, op):
        groups = _lane_groups(x)
        out = groups[0]
        for g in groups[1:]:
            out = op(out, g)
        return out

    def step(cur):
        prv = 1 - cur
        q = q_ref[0]
        zero = jnp.zeros_like(q)
        q2_scr[:T] = jnp.where(lane < DA_HEAD_DIM, q, zero)
        q2_scr[T:] = jnp.where(lane >= DA_HEAD_DIM, q, zero)
        mx_scr[...] = jnp.full(mx_scr.shape, -jnp.inf, jnp.float32)

        l0 = jnp.sum(l_scr[cur, 0], axis=-1, keepdims=True)
        l1 = jnp.sum(l_scr[cur, 1], axis=-1, keepdims=True)
        o = acc_scr[:T] / l0 - lam_ref[0] * (acc_scr[T:] / l1)
        ms = jnp.mean(o * o, axis=-1, keepdims=True)
        o = o * lax.rsqrt(ms + SUBLN_EPS) * sw_ref[...] * out_scale
        o_ref[0] = (o * _silu(z_ref[0])).astype(jnp.bfloat16)
        l_scr[prv] = jnp.zeros((2, T, LANES), jnp.float32)

        def stage1(c):
            bias = bias_ref[0, jnp.clip(c - c0, DA_D_LO, DA_D_HI) - DA_D_LO]
            s2 = lax.dot_general(q2_scr[...], k_ref[0, c * TK:(c + 1) * TK, :], _NT,
                                 preferred_element_type=jnp.float32)
            for m in range(2):
                s = s2[m * T:(m + 1) * T] + bias
                s_scr[cur, m, c] = s
                mx_scr[m] = jnp.maximum(mx_scr[m], group_reduce(s, jnp.maximum))

        def stage2(c, acc):
            ps = []
            for m in range(2):
                p = jnp.exp2(s_scr[prv, m, c] - lane_tile(m_scr[prv, m]))
                l_scr[prv, m] = l_scr[prv, m] + group_reduce(p, jnp.add)
                ps.append(p.astype(jnp.bfloat16))
            return acc + jnp.dot(jnp.concatenate(ps, axis=0), v_ref[0, c * TK:(c + 1) * TK, :],
                                 preferred_element_type=jnp.float32)

        acc = jnp.zeros((2 * T, DA_V_DIM), jnp.float32)
        for c in range(n_chunks):
            stage1(c)
            acc = stage2(c, acc)
        for m in range(2):
            row_max = jnp.max(mx_scr[m], axis=-1, keepdims=True)
            m_scr[cur, m] = jnp.broadcast_to(row_max, (T, LANES))
        acc_scr[...] = acc

    parity = lax.rem(t, 2)
    for cur in range(2):
        pl.when(parity == cur)(functools.partial(step, cur))


def _diff_attention(proj, lam, bias_tiles, subln_w, lam_init):
    B, S, _ = proj.shape
    T, TK = DA_TQ, DA_TK
    assert S % T == 0 and T % TK == 0
    n = S // TK
    nq = S // T
    n_blocks = B * DA_HEADS * nq

    def block(t, lag):
        u = jnp.clip(t - lag, 0, n_blocks - 1)
        return lax.div(u, DA_HEADS * nq), lax.rem(lax.div(u, nq), DA_HEADS), lax.rem(u, nq)

    def q_map(t):
        b, h, i = block(t, 0)
        return b, i, _QA + h

    def k_map(t):
        b, h, _ = block(t, 0)
        return b, 0, _KA + h

    def v_map(t):
        b, h, _ = block(t, 1)
        return b, 0, _VA + h

    def bias_map(t):
        _, h, _ = block(t, 0)
        return h, 0, 0, 0

    def z_map(t):
        b, h, i = block(t, 2)
        return b, i, _ZA + h

    def o_map(t):
        b, h, i = block(t, 2)
        return b, i, h

    return pl.pallas_call(
        functools.partial(_da_kernel, n_chunks=n, n_blocks=n_blocks, out_scale=1.0 - lam_init),
        grid=(n_blocks + 2,),
        in_specs=[
            pl.BlockSpec(memory_space=pltpu.SMEM),
            pl.BlockSpec((1, T, LANES), q_map),
            pl.BlockSpec((1, S, LANES), k_map),
            pl.BlockSpec((1, S, LANES), v_map),
            pl.BlockSpec((1, DA_BIAS_TILES, T, TK), bias_map),
            pl.BlockSpec((1, DA_V_DIM), lambda t: (0, 0)),
            pl.BlockSpec((1, T, LANES), z_map),
        ],
        out_specs=pl.BlockSpec((1, T, LANES), o_map),
        out_shape=jax.ShapeDtypeStruct((B, S, DA_WIDTH), jnp.bfloat16),
        scratch_shapes=[
            pltpu.VMEM((2, 2, n, T, TK), jnp.float32),
            pltpu.VMEM((2 * T, LANES), jnp.bfloat16),
            pltpu.VMEM((2, T, LANES), jnp.float32),
            pltpu.VMEM((2, 2, T, LANES), jnp.float32),
            pltpu.VMEM((2, 2, T, LANES), jnp.float32),
            pltpu.VMEM((2 * T, DA_V_DIM), jnp.float32),
        ],
        compiler_params=_params(("arbitrary",)),
        name="diff_attn",
    )(lam, proj, proj, proj, bias_tiles, subln_w, proj)


def _na_kernel(q_ref, k_ref, v_ref, bias_ref, o_ref, *, rows):
    rb = pl.program_id(1)
    win = NA_ROWS * GRID_W
    gw = NA_GROUP * NA_HEAD_DIM
    head_of_lane = lax.broadcasted_iota(jnp.int32, (GRID_W, gw), 1) // NA_HEAD_DIM

    def row_body(i, carry):
        r = rb * NA_RB + i
        start = jnp.clip(r - NA_ROWS // 2, 0, rows - NA_ROWS)
        var = r - start
        krows = pl.ds(pl.multiple_of(start * GRID_W, GRID_W), win)
        qrows = pl.ds(pl.multiple_of(i * GRID_W, GRID_W), GRID_W)
        for g in range(NA_HEADS // NA_GROUP):
            cols = slice(g * gw, (g + 1) * gw)
            qg = q_ref[0, qrows, cols]
            zero = jnp.zeros_like(qg)
            qbd = jnp.concatenate([jnp.where(head_of_lane == h, qg, zero) for h in range(NA_GROUP)],
                                  axis=0)
            s = lax.dot_general(qbd, k_ref[0, krows, cols], _NT, preferred_element_type=jnp.float32)
            s = s + bias_ref[var, g]
            m = jnp.max(s, axis=-1, keepdims=True)
            p = jnp.exp2(s - m)
            l = jnp.sum(p, axis=-1, keepdims=True)
            pv = jnp.dot(p.astype(jnp.bfloat16), v_ref[0, krows, cols],
                         preferred_element_type=jnp.float32) / l
            o = pv[:GRID_W]
            for h in range(1, NA_GROUP):
                o = jnp.where(head_of_lane == h, pv[h * GRID_W:(h + 1) * GRID_W], o)
            o_ref[0, qrows, cols] = o.astype(jnp.bfloat16)
        return carry

    lax.fori_loop(0, NA_RB, row_body, 0, unroll=True)


def _neighbourhood_attention(proj, na_bias):
    B, S, _ = proj.shape
    rows = S // GRID_W
    assert S % GRID_W == 0 and rows >= NA_ROWS and rows % NA_RB == 0
    blk = NA_RB * GRID_W
    return pl.pallas_call(
        functools.partial(_na_kernel, rows=rows),
        grid=(B, rows // NA_RB),
        in_specs=[
            pl.BlockSpec((1, blk, NA_WIDTH), lambda b, i: (b, i, _QN_BLK)),
            pl.BlockSpec((1, S, NA_WIDTH), lambda b, i: (b, 0, _KN_BLK)),
            pl.BlockSpec((1, S, NA_WIDTH), lambda b, i: (b, 0, _VN_BLK)),
            pl.BlockSpec(na_bias.shape, lambda b, i: (0, 0, 0, 0)),
        ],
        out_specs=pl.BlockSpec((1, blk, NA_WIDTH), lambda b, i: (b, i, 0)),
        out_shape=jax.ShapeDtypeStruct((B, S, NA_WIDTH), jnp.bfloat16),
        compiler_params=_params(("parallel", "parallel")),
        name="nbr_attn",
    )(proj, proj, proj, na_bias)


def _out_kernel(x_ref, ua_ref, on_ref, zn_ref, g_ref, woa_ref, won_ref, wout_ref, pw_ref, y_ref):
    def sigmoid(v):
        return 0.5 * jnp.tanh(0.5 * v) + 0.5

    ya = jnp.dot(ua_ref[...], woa_ref[...], preferred_element_type=jnp.float32)
    un = (on_ref[...].astype(jnp.float32) * _silu(zn_ref[...])).astype(jnp.bfloat16)
    yn = jnp.dot(un, won_ref[...], preferred_element_type=jnp.float32)
    ga = sigmoid(g_ref[:, :D_MODEL].astype(jnp.float32))
    gn = sigmoid(g_ref[:, D_MODEL:].astype(jnp.float32))
    merged = (ga * ya + gn * yn).astype(jnp.bfloat16)
    out = jnp.dot(merged, wout_ref[...], preferred_element_type=jnp.float32)
    ms = jnp.mean(out * out, axis=-1, keepdims=True)
    y_ref[...] = x_ref[...] + out * lax.rsqrt(ms + NORM_EPS) * pw_ref[...]


def _output(x2d, ua2d, on2d, proj2d, w_o_diff, w_o_na, w_out, post_w):
    tokens = x2d.shape[0]
    tm = OUT_TM
    assert tokens % tm == 0

    def const(shape):
        return pl.BlockSpec(shape, lambda i: (0, 0))

    return pl.pallas_call(
        _out_kernel,
        grid=(tokens // tm,),
        in_specs=[
            pl.BlockSpec((tm, D_MODEL), lambda i: (i, 0)),
            pl.BlockSpec((tm, DA_WIDTH), lambda i: (i, 0)),
            pl.BlockSpec((tm, NA_WIDTH), lambda i: (i, 0)),
            pl.BlockSpec((tm, NA_WIDTH), lambda i: (i, _ZN_BLK)),
            pl.BlockSpec((tm, 2 * D_MODEL), lambda i: (i, _G_BLK)),
            const((DA_WIDTH, D_MODEL)),
            const((NA_WIDTH, D_MODEL)),
            const((D_MODEL, D_MODEL)),
            const((1, D_MODEL)),
        ],
        out_specs=pl.BlockSpec((tm, D_MODEL), lambda i: (i, 0)),
        out_shape=jax.ShapeDtypeStruct((tokens, D_MODEL), jnp.float32),
        compiler_params=_params(("parallel",)),
        name="out_proj",
    )(x2d, ua2d, on2d, proj2d, proj2d, w_o_diff, w_o_na, w_out, post_w)


def _t5_bucket(rel):
    nb = T5_BUCKETS // 2
    ret = jnp.where(rel > 0, nb, 0)
    n = jnp.abs(rel)
    max_exact = nb // 2
    nf = jnp.maximum(n, 1).astype(jnp.float32)
    large = max_exact + (jnp.log(nf / max_exact) / math.log(T5_MAX_DIST / max_exact)
                         * (nb - max_exact)).astype(jnp.int32)
    large = jnp.minimum(large, nb - 1)
    return ret + jnp.where(n < max_exact, n, large)


def _toeplitz(w, rows, first, cols):
    M = w.shape[-1]
    assert first - (rows - 1) >= 0 and first + cols <= M - 1
    lead = w.shape[:-1]
    flat = jnp.broadcast_to(w[..., None, :], lead + (rows, M)).reshape(lead + (rows * M,))
    return flat[..., :rows * (M - 1)].reshape(lead + (rows, M - 1))[..., first:first + cols]


def _t5_tables(t5_rel_bias):
    T, TK = DA_TQ, DA_TK
    base = T - 1 - DA_D_LO * TK
    first = LANES - 1
    assert (base - first) % LANES == 0
    M = base + DA_D_HI * TK + TK + 8
    rel = jnp.arange(M, dtype=jnp.int32) - base
    vec = jnp.transpose(t5_rel_bias[_t5_bucket(rel)].astype(jnp.float32) * LOG2E)
    band = _toeplitz(vec, LANES, first, M - 1 - first)
    tiles = [jnp.concatenate(
        [band[:, :, base - first + d * TK - a * LANES:][:, :, :TK] for a in range(T // LANES)], axis=1)
        for d in range(DA_D_LO, DA_D_HI + 1)]
    return jnp.stack(tiles, axis=1)


def _na_tables(na_rpb):
    n_dr, n_dc = 2 * NA_ROWS - 1, 2 * NA_COLS - 1
    c = jnp.arange(GRID_W, dtype=jnp.int32)[:, None]
    kc = jnp.arange(GRID_W, dtype=jnp.int32)[None, :]
    cs = jnp.clip(c - NA_COLS // 2, 0, GRID_W - NA_COLS)
    rpb = na_rpb.astype(jnp.float32)[:, :, :, None, None] * LOG2E
    band = jnp.zeros((NA_HEADS, n_dr, GRID_W, GRID_W), jnp.float32)
    for dc in range(n_dc):
        band = jnp.where(kc - c + (NA_COLS - 1) == dc, rpb[:, :, dc], band)
    band = jnp.where((kc >= cs) & (kc < cs + NA_COLS), band, NEG_BIG)
    per_var = [jnp.transpose(band[:, NA_ROWS - 1 - v:n_dr - v], (0, 2, 1, 3)) for v in range(NA_ROWS)]
    return jnp.stack(per_var, axis=0).reshape(NA_ROWS, NA_HEADS // NA_GROUP, NA_GROUP * GRID_W,
                                              NA_ROWS * GRID_W)


def _layer(x, tables, pre_w, post_w, w_in, lam, subln_w, w_o_diff, w_o_na, w_out, lam_init):
    B, S, D = x.shape
    t5_tiles, na_bias = tables
    x2d = x.reshape(B * S, D)
    proj2d = _proj(x2d, pre_w, w_in)
    proj = proj2d.reshape(B, S, IN_WIDTH)
    oa = _diff_attention(proj, lam, t5_tiles, subln_w, lam_init)
    on = _neighbourhood_attention(proj, na_bias)
    y = _output(x2d, oa.reshape(B * S, DA_WIDTH), on.reshape(B * S, NA_WIDTH), proj2d,
                w_o_diff, w_o_na, w_out, post_w)
    return y.reshape(B, S, D)


def _trunk(xs, t5_rel_bias, pre_norm_w, post_norm_w, w_in, lambda_q1, lambda_k1, lambda_q2,
           lambda_k2, subln_w, na_rpb, w_o_diff, w_o_na, w_out):
    depth = w_in.shape[0]
    t5_tiles = _t5_tables(t5_rel_bias)
    f32 = jnp.float32
    for l in range(depth):
        lam_init = 0.8 - 0.6 * math.exp(-0.3 * l)
        lam = (jnp.exp(jnp.sum(lambda_q1[l].astype(f32) * lambda_k1[l].astype(f32)))
               - jnp.exp(jnp.sum(lambda_q2[l].astype(f32) * lambda_k2[l].astype(f32)))
               + lam_init).reshape(1)
        tables = (t5_tiles, _na_tables(na_rpb[l]))
        args = (pre_norm_w[l].reshape(1, D_MODEL).astype(f32),
                post_norm_w[l].reshape(1, D_MODEL).astype(f32),
                w_in[l].astype(jnp.bfloat16), lam,
                subln_w[l].reshape(1, DA_V_DIM).astype(f32),
                w_o_diff[l].astype(jnp.bfloat16), w_o_na[l].astype(jnp.bfloat16),
                w_out[l].astype(jnp.bfloat16), lam_init)
        xs = [_layer(x, tables, *args) for x in xs]
    return xs


def kernel(x_prompt, x_sample, t5_rel_bias, pre_norm_w, post_norm_w, w_in, lambda_q1, lambda_k1,
           lambda_q2, lambda_k2, subln_w, na_rpb, w_o_diff, w_o_na, w_out):
    y_prompt, y_sample = _trunk([x_prompt, x_sample], t5_rel_bias, pre_norm_w, post_norm_w, w_in,
                                lambda_q1, lambda_k1, lambda_q2, lambda_k2, subln_w, na_rpb,
                                w_o_diff, w_o_na, w_out)
    return (y_prompt, y_sample)
```
